```python
import math
import jax, jax.numpy as jnp
from jax import lax
import numpy as np

D_MODEL = 1024
BATCH = 2
SEQ = 16384
DEPTH = 4

HEAD_DIM = 64
FOX_W = D_MODEL // 2
RWKV_W = D_MODEL // 4
RET_W = D_MODEL // 4
MIX_W = FOX_W + RWKV_W + RET_W
FOX_HEADS = FOX_W // HEAD_DIM
RWKV_HEADS = RWKV_W // HEAD_DIM
RET_HEADS = RET_W // HEAD_DIM
Q_BLOCK = 128
RET_CHUNK = 128
W_LORA = 64
A_LORA = 64
G_LORA = 128
LORA_W = W_LORA + A_LORA + G_LORA
ROPE_BASE = 10000.0
IN_SPLITS = (FOX_W, FOX_W, FOX_W, FOX_HEADS,
             RWKV_W, RWKV_W, RWKV_W, W_LORA, A_LORA, G_LORA,
             RET_W, RET_W, RET_W, RET_W)
IN_W = sum(IN_SPLITS)
N_GROUPS = 4
EXPERTS_PER_GROUP = 8
N_EXPERTS = N_GROUPS * EXPERTS_PER_GROUP
TOP_K = 2
D_EXPERT = 512
MOE_BLOCK = 128
ALPHA = (2.0 * DEPTH) ** 0.25
BETA = (8.0 * DEPTH) ** -0.25
LN_EPS = 1e-5
RWKV_GN_EPS = 64e-5
RET_GN_EPS = 1e-6

kernel_name = "hybrid_fox_rwkv7_retnet_hmoe_deepnorm"

F32 = jnp.float32


def split_columns(p, sizes):
    outs = []
    start = 0
    for s in sizes:
        outs.append(p[..., start:start + s])
        start += s
    return outs


def layer_norm(x, g, b):
    xf = x.astype(F32)
    mu = jnp.mean(xf, -1, keepdims=True)
    var = jnp.mean(jnp.square(xf - mu), -1, keepdims=True)
    return ((xf - mu) * lax.rsqrt(var + LN_EPS) * g + b).astype(x.dtype)


def head_norm(y, eps):
    yf = y.astype(F32)
    mu = jnp.mean(yf, -1, keepdims=True)
    var = jnp.mean(jnp.square(yf - mu), -1, keepdims=True)
    return (yf - mu) * lax.rsqrt(var + eps)


def fox_attention(q, k, v, f_logit):
    B_, S_, H, Dh = q.shape
    c = jnp.cumsum(jax.nn.log_sigmoid(f_logit.astype(F32)), axis=1)
    c = c.transpose(0, 2, 1)
    qh = q.transpose(0, 2, 1, 3)
    kh = k.transpose(0, 2, 1, 3)
    vh = v.transpose(0, 2, 1, 3)
    scale = Dh ** -0.5
    kpos = jnp.arange(S_)
    n_blk = S_ // Q_BLOCK

    def one_block(i):
        start = i * Q_BLOCK
        qb = lax.dynamic_slice_in_dim(qh, start, Q_BLOCK, axis=2)
        cb = lax.dynamic_slice_in_dim(c, start, Q_BLOCK, axis=2)
        s = jnp.einsum('bhqd,bhkd->bhqk', qb, kh).astype(F32) * scale
        s = s + cb[..., :, None] - c[..., None, :]
        qpos = start + jnp.arange(Q_BLOCK)
        s = jnp.where(kpos[None, :] <= qpos[:, None], s, -jnp.inf)
        p = jax.nn.softmax(s, axis=-1).astype(vh.dtype)
        return jnp.einsum('bhqk,bhkd->bhqd', p, vh)

    out = lax.map(one_block, jnp.arange(n_blk))
    out = out.transpose(1, 0, 3, 2, 4)
    return out.reshape(B_, S_, H * Dh)


def rwkv7_time_mix(r, k, v, wd, ad, gd, mu_rkv, mu_lora, w0, w2, a0, a2, g2,
                   k_k, k_a, r_k, ln_g, ln_b):
    B_, S_, _ = r.shape
    H, N = RWKV_HEADS, HEAD_DIM

    def shift_mix(p, mu):
        prev = jnp.pad(p, ((0, 0), (1, 0), (0, 0)))[:, :-1]
        return p + (prev - p) * mu

    r = shift_mix(r, mu_rkv[0]).astype(F32)
    k = shift_mix(k, mu_rkv[1]).astype(F32)
    v = shift_mix(v, mu_rkv[2]).astype(F32)
    lora = shift_mix(jnp.concatenate([wd, ad, gd], axis=-1), mu_lora).astype(F32)
    wd, ad, gd = split_columns(lora, (W_LORA, A_LORA, G_LORA))
    log_w = -jax.nn.softplus(-(w0 + jnp.tanh(wd) @ w2)) - 0.5
    decay = jnp.exp(-jnp.exp(log_w))
    a = jax.nn.sigmoid(a0 + ad @ a2)
    g = jax.nn.sigmoid(gd) @ g2
    kk = (k * k_k).reshape(B_, S_, H, N)
    kk = kk / jnp.maximum(jnp.sqrt(jnp.sum(kk * kk, -1, keepdims=True)), 1e-12)
    k = k * (1.0 + (a - 1.0) * k_a)
    rh = r.reshape(B_, S_, H, N)
    kh = k.reshape(B_, S_, H, N)
    vh = v.reshape(B_, S_, H, N)
    wh = decay.reshape(B_, S_, H, N)
    ah = a.reshape(B_, S_, H, N)

    def step(state, inp):
        r_t, w_t, k_t, v_t, kk_t, a_t = inp
        sa = jnp.einsum('bhvk,bhk->bhv', state, -kk_t)
        state = (state * w_t[:, :, None, :]
                 + sa[..., None] * (kk_t * a_t)[:, :, None, :]
                 + v_t[..., None] * k_t[:, :, None, :])
        return state, jnp.einsum('bhvk,bhk->bhv', state, r_t)

    xs = (jnp.moveaxis(rh, 1, 0), jnp.moveaxis(wh, 1, 0), jnp.moveaxis(kh, 1, 0),
          jnp.moveaxis(vh, 1, 0), jnp.moveaxis(kk, 1, 0), jnp.moveaxis(ah, 1, 0))
    s0 = jnp.zeros((B_, H, N, N), F32)
    _, y = lax.scan(step, s0, xs)
    y = jnp.moveaxis(y, 0, 1)
    y = head_norm(y, RWKV_GN_EPS) * ln_g.reshape(H, N) + ln_b.reshape(H, N)
    y = y + jnp.sum(rh * kh * r_k, -1, keepdims=True) * vh
    return y.reshape(B_, S_, RWKV_W) * g


def rotary(x, pos):
    d = x.shape[-1]
    half = d // 2
    inv = ROPE_BASE ** (-jnp.arange(half, dtype=F32) / half)
    ang = pos[:, None] * inv[None, :]
    cos, sin = jnp.cos(ang), jnp.sin(ang)
    xf = x.astype(F32)
    x1, x2 = xf[..., :half], xf[..., half:]
    return jnp.concatenate([x1 * cos - x2 * sin, x1 * sin + x2 * cos], axis=-1)


def retention(q, k, v, g, gn_g):
    B_, S_, _ = q.shape
    H, d, C = RET_HEADS, HEAD_DIM, RET_CHUNK
    n_chunk = S_ // C
    to_heads = lambda t: t.reshape(B_, S_, H, d).transpose(0, 2, 1, 3)
    pos = jnp.arange(S_, dtype=F32)
    qh = rotary(to_heads(q), pos)
    kh = rotary(to_heads(k), pos) * (d ** -0.5)
    vh = to_heads(v).astype(F32)
    qc = qh.reshape(B_, H, n_chunk, C, d)
    kc = kh.reshape(B_, H, n_chunk, C, d)
    vc = vh.reshape(B_, H, n_chunk, C, d)
    log_gamma = jnp.log1p(-jnp.exp2(-5.0 - jnp.arange(H, dtype=F32)))
    idx = jnp.arange(C, dtype=F32)
    diff = idx[:, None] - idx[None, :]
    dmask = jnp.where(diff >= 0, jnp.exp(jnp.maximum(diff, 0.0) * log_gamma[:, None, None]), 0.0)
    scores = jnp.einsum('bhncd,bhnsd->bhncs', qc, kc) * dmask[None, :, None]
    y_intra = jnp.einsum('bhncs,bhnse->bhnce', scores, vc)
    k_dec = jnp.exp((C - 1.0 - idx)[None, :] * log_gamma[:, None])
    q_dec = jnp.exp((idx + 1.0)[None, :] * log_gamma[:, None])
    kv = jnp.einsum('bhncd,bhnce->bhnde', kc * k_dec[None, :, None, :, None], vc)
    chunk_decay = jnp.exp(C * log_gamma)[None, :, None, None]

    def step(R, kv_n):
        return R * chunk_decay + kv_n, R

    _, R_prev = lax.scan(step, jnp.zeros((B_, H, d, d), F32), jnp.moveaxis(kv, 2, 0))
    R_prev = jnp.moveaxis(R_prev, 0, 2)
    y_cross = jnp.einsum('bhncd,bhnde->bhnce', qc * q_dec[None, :, None, :, None], R_prev)
    y = (y_intra + y_cross).reshape(B_, H, S_, d).transpose(0, 2, 1, 3)
    y = head_norm(y, RET_GN_EPS) * gn_g.reshape(H, d)
    return y.reshape(B_, S_, RET_W) * jax.nn.silu(g.astype(F32))


def hierarchical_moe(h, w_group, b_group, w_expert, b_expert, w1, w3, w2):
    B_, S_, D = h.shape
    T = B_ * S_
    hf = h.reshape(T, D)
    g_logits = (hf @ w_group + b_group).astype(F32)
    grp = jnp.argmax(g_logits, axis=-1)
    g_gate = jnp.take_along_axis(jax.nn.softmax(g_logits, -1), grp[:, None], axis=-1)
    e_logits = (hf @ w_expert + b_expert).astype(F32).reshape(T, N_GROUPS, EXPERTS_PER_GROUP)
    e_sel = jnp.take_along_axis(e_logits, grp[:, None, None], axis=1)[:, 0]
    top_val, top_idx = lax.top_k(e_sel, TOP_K)
    gates = g_gate * jax.nn.softmax(top_val, axis=-1)
    expert_id = grp[:, None] * EXPERTS_PER_GROUP + top_idx

    A = T * TOP_K
    eid = expert_id.reshape(A)
    tok = jnp.repeat(jnp.arange(T), TOP_K)
    gate = gates.reshape(A)
    order = jnp.argsort(eid)
    e_sorted, tok_sorted, gate_sorted = eid[order], tok[order], gate[order]
    counts = jnp.zeros((N_EXPERTS,), jnp.int32).at[eid].add(1)
    offsets = jnp.cumsum(counts) - counts
    padded = ((counts + MOE_BLOCK - 1) // MOE_BLOCK) * MOE_BLOCK
    pad_end = jnp.cumsum(padded)
    pad_off = pad_end - padded
    dest = pad_off[e_sorted] + (jnp.arange(A) - offsets[e_sorted])
    n_blk = A // MOE_BLOCK + N_EXPERTS
    x_pad = jnp.zeros((n_blk * MOE_BLOCK, D), h.dtype).at[dest].set(hf[tok_sorted])
    block_expert = jnp.minimum(
        jnp.searchsorted(pad_end, jnp.arange(n_blk) * MOE_BLOCK, side='right'), N_EXPERTS - 1)

    def expert_block(args):
        xb, e = args
        return (jax.nn.silu(xb @ w1[e]) * (xb @ w3[e])) @ w2[e]

    y_pad = lax.map(expert_block, (x_pad.reshape(n_blk, MOE_BLOCK, D), block_expert))
    y = y_pad.reshape(n_blk * MOE_BLOCK, D)[dest] * gate_sorted[:, None].astype(h.dtype)
    return jax.ops.segment_sum(y, tok_sorted, num_segments=T).reshape(B_, S_, D)


def setup_inputs(seed: int = 0) -> dict:
    key = jax.random.key(seed)
    ks = iter(jax.random.split(key, 32))
    L, D = DEPTH, D_MODEL

    def nrm(shape, scale):
        return scale * jax.random.normal(next(ks), shape, F32)

    def uni(shape):
        return jax.random.uniform(next(ks), shape, F32)

    return {
        "x": nrm((BATCH, SEQ, D), 1.0),
        "w_in": nrm((L, D, IN_W), D ** -0.5),
        "fox_fgate_bias": jnp.linspace(1.0, 6.0, FOX_HEADS, dtype=F32)[None, :] + nrm((L, FOX_HEADS), 0.1),
        "rwkv_mu_rkv": uni((L, 3, RWKV_W)),
        "rwkv_mu_lora": uni((L, LORA_W)),
        "rwkv_w0": jnp.linspace(-6.0, -1.0, RWKV_W, dtype=F32)[None, :] + nrm((L, RWKV_W), 0.1),
        "rwkv_w2": nrm((L, W_LORA, RWKV_W), 0.5 * W_LORA ** -0.5),
        "rwkv_a0": nrm((L, RWKV_W), 0.1),
        "rwkv_a2": nrm((L, A_LORA, RWKV_W), A_LORA ** -0.5),
        "rwkv_g2": nrm((L, G_LORA, RWKV_W), G_LORA ** -0.5),
        "rwkv_k_k": 0.85 + nrm((L, RWKV_W), 0.05),
        "rwkv_k_a": 1.0 + nrm((L, RWKV_W), 0.05),
        "rwkv_r_k": nrm((L, RWKV_HEADS, HEAD_DIM), 0.1),
        "rwkv_ln_g": 1.0 + nrm((L, RWKV_W), 0.05),
        "rwkv_ln_b": nrm((L, RWKV_W), 0.02),
        "ret_gn_g": 1.0 + nrm((L, RET_W), 0.05),
        "w_out": nrm((L, MIX_W, D), BETA * MIX_W ** -0.5),
        "ln1_g": 1.0 + nrm((L, D), 0.05),
        "ln1_b": nrm((L, D), 0.02),
        "ln2_g": 1.0 + nrm((L, D), 0.05),
        "ln2_b": nrm((L, D), 0.02),
        "moe_w_group": nrm((L, D, N_GROUPS), D ** -0.5),
        "moe_b_group": nrm((L, N_GROUPS), 0.01),
        "moe_w_expert": nrm((L, D, N_EXPERTS), D ** -0.5),
        "moe_b_expert": nrm((L, N_EXPERTS), 0.01),
        "moe_w1": nrm((L, N_EXPERTS, D, D_EXPERT), D ** -0.5),
        "moe_w3": nrm((L, N_EXPERTS, D, D_EXPERT), D ** -0.5),
        "moe_w2": nrm((L, N_EXPERTS, D_EXPERT, D), BETA * D_EXPERT ** -0.5),
    }


def reference(x, w_in, fox_fgate_bias, rwkv_mu_rkv, rwkv_mu_lora, rwkv_w0, rwkv_w2,
              rwkv_a0, rwkv_a2, rwkv_g2, rwkv_k_k, rwkv_k_a, rwkv_r_k, rwkv_ln_g, rwkv_ln_b,
              ret_gn_g, w_out, ln1_g, ln1_b, ln2_g, ln2_b, moe_w_group, moe_b_group,
              moe_w_expert, moe_b_expert, moe_w1, moe_w3, moe_w2):
    B_, S_, D = x.shape
    for l in range(DEPTH):
        p = x @ w_in[l]
        (fq, fk, fv, ff, rr, rk, rv, rwd, rad, rgd, tq, tk, tv, tg) = split_columns(p, IN_SPLITS)
        fox_heads = lambda t: t.reshape(B_, S_, FOX_HEADS, HEAD_DIM)
        y_fox = fox_attention(fox_heads(fq), fox_heads(fk), fox_heads(fv),
                              ff + fox_fgate_bias[l])
        y_rwkv = rwkv7_time_mix(rr, rk, rv, rwd, rad, rgd, rwkv_mu_rkv[l], rwkv_mu_lora[l],
                                rwkv_w0[l], rwkv_w2[l], rwkv_a0[l], rwkv_a2[l], rwkv_g2[l],
                                rwkv_k_k[l], rwkv_k_a[l], rwkv_r_k[l], rwkv_ln_g[l], rwkv_ln_b[l])
        y_ret = retention(tq, tk, tv, tg, ret_gn_g[l])
        mixed = jnp.concatenate([y_fox.astype(x.dtype), y_rwkv.astype(x.dtype),
                                 y_ret.astype(x.dtype)], axis=-1) @ w_out[l]
        x = layer_norm(ALPHA * x + mixed, ln1_g[l], ln1_b[l])
        moe = hierarchical_moe(x, moe_w_group[l], moe_b_group[l], moe_w_expert[l],
                               moe_b_expert[l], moe_w1[l], moe_w3[l], moe_w2[l])
        x = layer_norm(ALPHA * x + moe, ln2_g[l], ln2_b[l])
    return x
```

```python
import functools
import math

import jax
import jax.numpy as jnp
import numpy as np
from jax import lax
from jax.experimental import pallas as pl
from jax.experimental.pallas import tpu as pltpu

F32 = jnp.float32
BF16 = jnp.bfloat16
HIGHEST = lax.Precision.HIGHEST

D_MODEL = 1024
DEPTH = 4
HEAD_DIM = 64
FOX_W, RWKV_W, RET_W = 512, 256, 256
FOX_HEADS, RWKV_HEADS, RET_HEADS = 8, 4, 4
W_LORA, A_LORA, G_LORA = 64, 64, 128
ROPE_BASE = 10000.0
N_GROUPS, EXPERTS_PER_GROUP, N_EXPERTS, D_EXPERT = 4, 8, 32, 512
ALPHA = (2.0 * DEPTH) ** 0.25
LN_EPS = 1e-5
RWKV_GN_EPS = 64e-5
RET_GN_EPS = 1e-6

LANES = 128
VMEM_LIMIT = 56 * 1024 * 1024

PROJ_TM = 512
ATT_T = 512
RWKV_TS = 512
RWKV_C = 64
RET_C = 256
OUT_TM = 512
RANK_TM = 512
MOE_TM = 256
COMB_TM = 256

P_FOX = 0
P_FG = 1536
P_RWKV = 1664
P_RET = 2688
P_TOT = 3712


def _cparams(sem):
    return pltpu.CompilerParams(dimension_semantics=sem, vmem_limit_bytes=VMEM_LIMIT)


def _dot(a, b, precision=None):
    return jnp.dot(a, b, preferred_element_type=F32, precision=precision)


def _dot_nt(a, b, precision=None):
    return lax.dot_general(a, b, (((1,), (1,)), ((), ())), preferred_element_type=F32,
                           precision=precision)


def _log_sigmoid(z):
    return -(jnp.maximum(-z, 0.0) + jnp.log(1.0 + jnp.exp(-jnp.abs(z))))


def _sigmoid(z):
    return 1.0 / (1.0 + jnp.exp(-z))


def _iota(shape, dim):
    return lax.broadcasted_iota(jnp.int32, shape, dim)


def _proj_kernel(x_ref, w_ref, fb_ref, tri_ref, qa_ref, ka_ref, va_ref, rw_ref, rt_ref, carry_ref):
    tm = x_ref.shape[1]

    @pl.when(pl.program_id(1) == 0)
    def _():
        carry_ref[...] = jnp.zeros_like(carry_ref)

    xb = x_ref[0].astype(BF16)
    p_fox = _dot(xb, w_ref[:, P_FOX:P_FG])
    z = _dot(xb, w_ref[:, P_FG:P_RWKV]) + fb_ref[...]
    rw_ref[0] = _dot(xb, w_ref[:, P_RWKV:P_RET])
    rt_ref[0] = _dot(xb, w_ref[:, P_RET:P_TOT])

    c = _dot(tri_ref[...], _log_sigmoid(z), HIGHEST) + carry_ref[0:1, :]
    carry_ref[...] = jnp.broadcast_to(c[tm - 1:tm, :], carry_ref.shape)

    lane = _iota((tm, LANES), 1)
    for h in range(FOX_HEADS):
        cb = jnp.broadcast_to(c[:, h:h + 1], (tm, LANES))
        hi = cb.astype(BF16).astype(F32)
        r1 = cb - hi
        mid = r1.astype(BF16).astype(F32)
        lo = r1 - mid
        tile = (h // 2) * LANES

        def head_lanes(base):
            t = p_fox[:, base + tile: base + tile + LANES]
            return pltpu.roll(t, 64, 1) if h % 2 else t

        q = head_lanes(0) * (HEAD_DIM ** -0.5)
        k = head_lanes(FOX_W)
        v = head_lanes(2 * FOX_W)
        qa = jnp.where(lane < 64, q, jnp.where(lane == 64, hi, jnp.where(lane == 65, mid,
             jnp.where(lane == 66, lo, jnp.where(lane < 70, 1.0, 0.0)))))
        ka = jnp.where(lane < 64, k, jnp.where(lane < 67, 1.0, jnp.where(lane == 67, -hi,
             jnp.where(lane == 68, -mid, jnp.where(lane == 69, -lo, 0.0)))))
        va = jnp.where(lane < 64, v, jnp.where(lane == 64, 1.0, 0.0))
        qa_ref[0, h] = qa.astype(BF16)
        ka_ref[0, h] = ka.astype(BF16)
        va_ref[0, h] = va.astype(BF16)


def _project(x, w_bf16, fbias, tri):
    B, S, D = x.shape
    tm = PROJ_TM
    aug = jax.ShapeDtypeStruct((B, FOX_HEADS, S, LANES), BF16)
    wide = jax.ShapeDtypeStruct((B, S, 1024), F32)
    aug_spec = pl.BlockSpec((1, FOX_HEADS, tm, LANES), lambda b, s: (b, 0, s, 0))
    wide_spec = pl.BlockSpec((1, tm, 1024), lambda b, s: (b, s, 0))
    return pl.pallas_call(
        _proj_kernel,
        grid=(B, S // tm),
        in_specs=[pl.BlockSpec((1, tm, D), lambda b, s: (b, s, 0)),
                  pl.BlockSpec((D, P_TOT), lambda b, s: (0, 0)),
                  pl.BlockSpec((1, LANES), lambda b, s: (0, 0)),
                  pl.BlockSpec((tm, tm), lambda b, s: (0, 0))],
        out_specs=[aug_spec, aug_spec, aug_spec, wide_spec, wide_spec],
        out_shape=[aug, aug, aug, wide, wide],
        scratch_shapes=[pltpu.VMEM((8, LANES), F32)],
        compiler_params=_cparams(("arbitrary", "arbitrary")),
        name="proj",
    )(x, w_bf16, fbias, tri)


def _fox_kernel(qi_ref, ki_ref, q_ref, k_ref, v_ref, o_ref, m_ref, acc_ref):
    p = pl.program_id(2)
    qi = qi_ref[p]
    ki = ki_ref[p]
    t = q_ref.shape[2]

    @pl.when(ki == 0)
    def _():
        m_ref[...] = jnp.full_like(m_ref, -jnp.inf)
        acc_ref[...] = jnp.zeros_like(acc_ref)

    def step(masked):
        for j in range(2):
            s = _dot_nt(q_ref[0, j], k_ref[0, j])
            if masked:
                s = jnp.where(_iota((t, t), 1) <= _iota((t, t), 0), s, -jnp.inf)
            m_prev = m_ref[j]
            m_new = jnp.maximum(m_prev, jnp.max(s, axis=1, keepdims=True))
            alpha = jnp.exp(m_prev - m_new)
            pm = jnp.exp(s - m_new[:, 0:1])
            acc_ref[j] = alpha * acc_ref[j] + _dot(pm.astype(BF16), v_ref[0, j])
            m_ref[j] = m_new

    @pl.when(ki < qi)
    def _():
        step(False)

    @pl.when(ki == qi)
    def _():
        step(True)
        outs = []
        for j in range(2):
            acc = acc_ref[j]
            outs.append(acc / acc[:, 64:65])
        lane = _iota((t, LANES), 1)
        o_ref[0] = jnp.where(lane < 64, outs[0], pltpu.roll(outs[1], 64, 1)).astype(o_ref.dtype)


def _fox_attention(qa, ka, va):
    B, H, S, _ = qa.shape
    t = ATT_T
    nq = S // t
    qi_arr = np.concatenate([np.full(i + 1, i, np.int32) for i in range(nq)])
    ki_arr = np.concatenate([np.arange(i + 1, dtype=np.int32) for i in range(nq)])
    n_pairs = qi_arr.shape[0]
    grid_spec = pltpu.PrefetchScalarGridSpec(
        num_scalar_prefetch=2,
        grid=(B, H // 2, n_pairs),
        in_specs=[pl.BlockSpec((1, 2, t, LANES), lambda b, hp, p, qi, ki: (b, hp, qi[p], 0)),
                  pl.BlockSpec((1, 2, t, LANES), lambda b, hp, p, qi, ki: (b, hp, ki[p], 0)),
                  pl.BlockSpec((1, 2, t, LANES), lambda b, hp, p, qi, ki: (b, hp, ki[p], 0))],
        out_specs=pl.BlockSpec((1, t, LANES), lambda b, hp, p, qi, ki: (b, qi[p], hp)),
        scratch_shapes=[pltpu.VMEM((2, t, LANES), F32), pltpu.VMEM((2, t, LANES), F32)],
    )
    return pl.pallas_call(
        _fox_kernel,
        grid_spec=grid_spec,
        out_shape=jax.ShapeDtypeStruct((B, S, FOX_W), BF16),
        compiler_params=_cparams(("arbitrary", "arbitrary", "arbitrary")),
        name="fox_attn",
    )(jnp.asarray(qi_arr), jnp.asarray(ki_arr), qa, ka, va)


_RV_W0, _RV_A0, _RV_KK, _RV_KA, _RV_RK, _RV_LNG, _RV_LNB = range(7)


def _head_stack(x, n_heads):
    lane = _iota(x.shape, 1)
    return jnp.concatenate(
        [jnp.where(lane // HEAD_DIM == h, x, 0.0) for h in range(n_heads)], axis=0)


def _rwkv_kernel(p_ref, mu_ref, vec_ref, w2_ref, a2_ref, g2_ref, lt_ref, bo_ref, o_ref,
                 prev_ref, s_ref, at_ref, rt_ref, bt_ref, kt_ref, v_ref, wl_ref, y_ref, bon_ref, g_ref):
    nb, ts, _ = p_ref.shape
    C = RWKV_C
    H = RWKV_HEADS
    W = RWKV_W

    @pl.when(pl.program_id(0) == 0)
    def _():
        prev_ref[...] = jnp.zeros_like(prev_ref)
        s_ref[...] = jnp.zeros_like(s_ref)

    vec = vec_ref[...]
    row = lambda i: vec[i:i + 1, :]
    bo = bo_ref[...]

    for b in range(nb):
        p = p_ref[b]
        rowi = _iota(p.shape, 0)
        prev = jnp.where(rowi == 0, jnp.broadcast_to(prev_ref[b, 0:1, :], p.shape), pltpu.roll(p, 1, 0))
        prev_ref[b] = jnp.broadcast_to(p[ts - 1:ts, :], prev_ref.shape[1:])
        xm = p + (prev - p) * mu_ref[...]
        r, k, v = xm[:, 0:W], xm[:, W:2 * W], xm[:, 2 * W:3 * W]
        wd = xm[:, 3 * W:3 * W + W_LORA]
        ad = xm[:, 3 * W + W_LORA:3 * W + W_LORA + A_LORA]
        gd = xm[:, 3 * W + W_LORA + A_LORA:]
        lw = row(_RV_W0) + _dot(jnp.tanh(wd), w2_ref[...], HIGHEST)
        logdec = -jnp.exp(_log_sigmoid(lw) - 0.5)
        a = _sigmoid(row(_RV_A0) + _dot(ad, a2_ref[...], HIGHEST))
        g_ref[b] = _dot(_sigmoid(gd), g2_ref[...], HIGHEST)
        kk = k * row(_RV_KK)
        kk = kk / jnp.maximum(jnp.sqrt(_dot(kk * kk, bo, HIGHEST)), 1e-12)
        k2 = k * (1.0 + (a - 1.0) * row(_RV_KA))
        cum = _dot(lt_ref[...], logdec, HIGHEST)
        wincl = jnp.exp(cum)
        winv = jnp.exp(-cum)
        at_ref[b] = -kk * jnp.exp(cum - logdec)
        rt_ref[b] = r * wincl
        bt_ref[b] = kk * a * winv
        kt_ref[b] = k2 * winv
        v_ref[b] = v
        wl_ref[b] = wincl
        bon_ref[b] = r * k2 * row(_RV_RK)

    li = _iota((C, H * C), 1) % C
    ri = _iota((C, H * C), 0)
    strict = li < ri
    incl = li <= ri
    strict_t = (_iota((H * C, C), 0) % C) < _iota((H * C, C), 1)
    bd = (_iota((W, W), 0) // HEAD_DIM) == (_iota((W, W), 1) // HEAD_DIM)
    pr_i = _iota((2 * C, LANES), 0)
    ins_row = jnp.where(_iota((2 * C, LANES), 1) < HEAD_DIM, pr_i, pr_i - C)
    zpair = jnp.zeros((2 * C, LANES), F32)

    def chunk_body(ci, carry):
        c0 = pl.multiple_of(ci * C, C)
        for b in range(nb):
            sl = pl.ds(c0, C)
            at, rt, bt, kt, vv = at_ref[b, sl, :], rt_ref[b, sl, :], bt_ref[b, sl, :], kt_ref[b, sl, :], v_ref[b, sl, :]
            s0 = s_ref[b]
            bstack, kstack, vstack = _head_stack(bt, H), _head_stack(kt, H), _head_stack(vv, H)
            aab_t = jnp.where(strict_t, _dot_nt(bstack, at, HIGHEST), 0.0)
            pak = jnp.where(strict, _dot_nt(at, kstack, HIGHEST), 0.0)
            prb = jnp.where(incl, _dot_nt(rt, bstack, HIGHEST), 0.0)
            prk = jnp.where(incl, _dot_nt(rt, kstack, HIGHEST), 0.0)
            base = _dot_nt(at, s0, HIGHEST) + _dot(pak, vstack, HIGHEST)
            u_pairs = []
            for hp in range(H // 2):
                a_pair = aab_t[2 * hp * C:(2 * hp + 2) * C, :]
                b_pair = base[:, hp * LANES:(hp + 1) * LANES]
                u = zpair
                for t in range(C):
                    coef = jnp.broadcast_to(a_pair[:, t:t + 1], (2 * C, LANES))
                    u_t = b_pair[t:t + 1, :] + jnp.sum(coef * u, axis=0, keepdims=True)
                    u = jnp.where(ins_row == t, jnp.broadcast_to(u_t, (2 * C, LANES)), u)
                u_pairs.append(u)
            ustack = jnp.concatenate([jnp.concatenate([u_pairs[0], zpair], axis=1),
                                      jnp.concatenate([zpair, u_pairs[1]], axis=1)], axis=0)
            u_full = jnp.concatenate([u_pairs[0][:C] + u_pairs[0][C:], u_pairs[1][:C] + u_pairs[1][C:]], axis=1)
            y_ref[b, sl, :] = (_dot_nt(rt, s0, HIGHEST) + _dot(prb, ustack, HIGHEST)
                               + _dot(prk, vstack, HIGHEST))
            upd = _dot(jnp.concatenate([u_full, vv], axis=0).T,
                       jnp.concatenate([bt, kt], axis=0), HIGHEST)
            wl = wl_ref[b, pl.ds(c0 + C - 1, 1), :]
            s_ref[b] = (s0 + jnp.where(bd, upd, 0.0)) * wl
        return carry

    lax.fori_loop(0, ts // C, chunk_body, 0)

    for b in range(nb):
        y = y_ref[b]
        mean = _dot(y, bo, HIGHEST) * (1.0 / HEAD_DIM)
        d = y - mean
        var = _dot(d * d, bo, HIGHEST) * (1.0 / HEAD_DIM)
        yn = d * lax.rsqrt(var + RWKV_GN_EPS) * row(_RV_LNG) + row(_RV_LNB)
        bonus = _dot(bon_ref[b], bo, HIGHEST) * v_ref[b]
        o_ref[b] = (yn + bonus) * g_ref[b]


def _rwkv(rw, prm, l):
    B, S, _ = rw.shape
    ts = RWKV_TS
    full = lambda a: pl.BlockSpec(a.shape, lambda s: (0,) * a.ndim)
    args = [prm["rwkv_mu"][l], prm["rwkv_vec"][l], prm["rwkv_w2"][l], prm["rwkv_a2"][l], prm["rwkv_g2"][l],
            prm["rwkv_lt"], prm["block_ones"]]
    big = lambda: pltpu.VMEM((B, ts, RWKV_W), F32)
    return pl.pallas_call(
        _rwkv_kernel,
        grid=(S // ts,),
        in_specs=[pl.BlockSpec((B, ts, 1024), lambda s: (0, s, 0))] + [full(a) for a in args],
        out_specs=pl.BlockSpec((B, ts, RWKV_W), lambda s: (0, s, 0)),
        out_shape=jax.ShapeDtypeStruct((B, S, RWKV_W), F32),
        scratch_shapes=[pltpu.VMEM((B, 8, 1024), F32), pltpu.VMEM((B, RWKV_W, RWKV_W), F32)]
                       + [big() for _ in range(9)],
        compiler_params=_cparams(("arbitrary",)),
        name="rwkv",
    )(rw, *args)


def _ret_kernel(p_ref, cos_ref, sin_ref, dm_ref, qd_ref, kd_ref, dmat_ref, bo_ref, gn_ref, o_ref, r_ref):
    C = p_ref.shape[1]
    W = RET_W
    H = RET_HEADS

    @pl.when(pl.program_id(1) == 0)
    def _():
        r_ref[...] = jnp.zeros_like(r_ref)

    p = p_ref[0]
    cos, sin = cos_ref[...], sin_ref[...]
    first_half = (_iota((C, W), 1) % HEAD_DIM) < (HEAD_DIM // 2)

    def rope(x):
        partner = jnp.where(first_half, pltpu.roll(x, W - HEAD_DIM // 2, 1), pltpu.roll(x, HEAD_DIM // 2, 1))
        return x * cos + partner * sin

    q = rope(p[:, 0:W])
    k = rope(p[:, W:2 * W]) * (HEAD_DIM ** -0.5)
    v = p[:, 2 * W:3 * W]
    g = p[:, 3 * W:4 * W]
    state = r_ref[...]
    scores = _dot_nt(q, _head_stack(k, H)) * dm_ref[...]
    y = _dot(scores, _head_stack(v, H)) + _dot(q * qd_ref[...], state)
    kv = _dot((k * kd_ref[...]).T, v)
    bd = (_iota((W, W), 0) // HEAD_DIM) == (_iota((W, W), 1) // HEAD_DIM)
    r_ref[...] = state * dmat_ref[...] + jnp.where(bd, kv, 0.0)

    bo = bo_ref[...]
    mean = _dot(y, bo, HIGHEST) * (1.0 / HEAD_DIM)
    d = y - mean
    var = _dot(d * d, bo, HIGHEST) * (1.0 / HEAD_DIM)
    o_ref[0] = d * lax.rsqrt(var + RET_GN_EPS) * gn_ref[...] * (g * _sigmoid(g))


def _retention(rt, prm, l):
    B, S, _ = rt.shape
    C = RET_C
    cst = lambda a: pl.BlockSpec(a.shape, lambda b, s: (0,) * a.ndim)
    consts = [prm["ret_dmask"], prm["ret_qdec"], prm["ret_kdec"], prm["ret_dmat"], prm["block_ones"],
              prm["ret_gn"][l]]
    return pl.pallas_call(
        _ret_kernel,
        grid=(B, S // C),
        in_specs=[pl.BlockSpec((1, C, 1024), lambda b, s: (b, s, 0)),
                  pl.BlockSpec((C, RET_W), lambda b, s: (s, 0)),
                  pl.BlockSpec((C, RET_W), lambda b, s: (s, 0))] + [cst(a) for a in consts],
        out_specs=pl.BlockSpec((1, C, RET_W), lambda b, s: (b, s, 0)),
        out_shape=jax.ShapeDtypeStruct((B, S, RET_W), F32),
        scratch_shapes=[pltpu.VMEM((RET_W, RET_W), F32)],
        compiler_params=_cparams(("arbitrary", "arbitrary")),
        name="retention",
    )(rt, prm["ret_cos"], prm["ret_sin"], *consts)


def _ret_tables(S):
    C, H, d = RET_C, RET_HEADS, HEAD_DIM
    half = d // 2
    inv = ROPE_BASE ** (-jnp.arange(half, dtype=F32) / half)
    ang = jnp.arange(S, dtype=F32)[:, None] * inv[None, :]
    cos, sin = jnp.cos(ang), jnp.sin(ang)
    cos_t = jnp.tile(jnp.concatenate([cos, cos], axis=1), (1, H))
    sin_t = jnp.tile(jnp.concatenate([-sin, sin], axis=1), (1, H))
    log_gamma = jnp.log1p(-jnp.exp2(-5.0 - jnp.arange(H, dtype=F32)))
    idx = jnp.arange(C, dtype=F32)
    diff = idx[:, None] - idx[None, :]
    dmask = jnp.where(diff >= 0, jnp.exp(jnp.maximum(diff, 0.0) * log_gamma[:, None, None]), 0.0)
    dmask = jnp.transpose(dmask, (1, 0, 2)).reshape(C, H * C)
    rep = lambda a: jnp.repeat(a, d, axis=1)
    kdec = rep(jnp.exp((C - 1.0 - idx)[:, None] * log_gamma[None, :]))
    qdec = rep(jnp.exp((idx + 1.0)[:, None] * log_gamma[None, :]))
    hid = np.arange(H * d) // d
    bd = jnp.asarray(hid[:, None] == hid[None, :])
    dmat = jnp.where(bd, jnp.repeat(jnp.exp(C * log_gamma), d)[:, None], 0.0)
    return {"ret_cos": cos_t, "ret_sin": sin_t, "ret_dmask": dmask, "ret_qdec": qdec, "ret_kdec": kdec,
            "ret_dmat": dmat}


def _layer_norm(z, g, b):
    mu = jnp.mean(z, axis=1, keepdims=True)
    d = z - mu
    var = jnp.mean(d * d, axis=1, keepdims=True)
    return d * lax.rsqrt(var + LN_EPS) * g + b


def _out_kernel(yf_ref, yr_ref, yt_ref, x_ref, w_ref, g_ref, b_ref, wr_ref, br_ref,
                x1_ref, info_ref, info_t_ref):
    tm = x_ref.shape[0]
    mixed = (_dot(yf_ref[...], w_ref[0:FOX_W, :])
             + _dot(yr_ref[...].astype(BF16), w_ref[FOX_W:FOX_W + RWKV_W, :])
             + _dot(yt_ref[...].astype(BF16), w_ref[FOX_W + RWKV_W:, :]))
    x1 = _layer_norm(ALPHA * x_ref[...] + mixed, g_ref[...], b_ref[...])
    x1_ref[...] = x1

    logits = _dot(x1, wr_ref[...], HIGHEST) + br_ref[...]
    lane = _iota((tm, LANES), 1).astype(F32)
    ninf = -jnp.inf
    first = lambda hit: jnp.min(jnp.where(hit, lane, float(LANES)), axis=1, keepdims=True)
    gl = jnp.where(lane < N_GROUPS, logits, ninf)
    gmax = jnp.max(gl, axis=1, keepdims=True)
    grp = first(gl == gmax)
    g_gate = 1.0 / jnp.sum(jnp.exp(gl - gmax), axis=1, keepdims=True)
    lo = N_GROUPS + grp * EXPERTS_PER_GROUP
    el = jnp.where(jnp.logical_and(lane >= lo, lane < lo + EXPERTS_PER_GROUP), logits, ninf)
    v1 = jnp.max(el, axis=1, keepdims=True)
    i1 = first(el == v1)
    el2 = jnp.where(lane == i1, ninf, el)
    v2 = jnp.max(el2, axis=1, keepdims=True)
    i2 = first(el2 == v2)
    e = jnp.exp(v2 - v1)
    den = 1.0 / (1.0 + e)
    info = jnp.where(lane == 0, i1 - N_GROUPS, jnp.where(lane == 1, i2 - N_GROUPS,
           jnp.where(lane == 2, g_gate * den, jnp.where(lane == 3, g_gate * e * den, 0.0))))
    info_ref[...] = info
    info_t_ref[...] = info.T[0:8, :]


def _out_proj(yf, yr, yt, x, prm, l):
    T = x.shape[0]
    tm = OUT_TM
    rows = lambda w: pl.BlockSpec((tm, w), lambda i: (i, 0))
    cst = lambda a: pl.BlockSpec(a.shape, lambda i: (0,) * a.ndim)
    consts = [prm["w_out"][l], prm["ln1_g"][l], prm["ln1_b"][l], prm["w_router"][l], prm["b_router"][l]]
    return pl.pallas_call(
        _out_kernel,
        grid=(T // tm,),
        in_specs=[rows(FOX_W), rows(RWKV_W), rows(RET_W), rows(D_MODEL)] + [cst(a) for a in consts],
        out_specs=[rows(D_MODEL), rows(LANES), pl.BlockSpec((8, tm), lambda i: (0, i))],
        out_shape=[jax.ShapeDtypeStruct((T, D_MODEL), F32), jax.ShapeDtypeStruct((T, LANES), F32),
                   jax.ShapeDtypeStruct((8, T), F32)],
        compiler_params=_cparams(("arbitrary",)),
        name="out_proj",
    )(yf, yr, yt, x, *consts)


def _rank_kernel(it_ref, tri_ref, rk_ref, cnt_ref, carry_ref):
    tm = it_ref.shape[1]

    @pl.when(pl.program_id(0) == 0)
    def _():
        carry_ref[...] = jnp.zeros_like(carry_ref)

    sub = _iota((N_EXPERTS, tm), 0).astype(F32)
    oh1 = sub == it_ref[0:1, :]
    oh2 = sub == it_ref[1:2, :]
    oh = jnp.where(jnp.logical_or(oh1, oh2), 1.0, 0.0)
    before = _dot(oh.astype(BF16), tri_ref[...]) + carry_ref[:, 0:1]
    r1 = jnp.sum(jnp.where(oh1, before, 0.0), axis=0, keepdims=True)
    r2 = jnp.sum(jnp.where(oh2, before, 0.0), axis=0, keepdims=True)
    rk_ref[...] = jnp.concatenate([r1, r2, jnp.zeros((6, tm), F32)], axis=0)
    carry_ref[...] = carry_ref[...] + jnp.sum(oh, axis=1, keepdims=True)
    cnt_ref[...] = carry_ref[...]


def _rank(info_t, tri):
    T = info_t.shape[1]
    tm = RANK_TM
    return pl.pallas_call(
        _rank_kernel,
        grid=(T // tm,),
        in_specs=[pl.BlockSpec((8, tm), lambda i: (0, i)), pl.BlockSpec((tm, tm), lambda i: (0, 0))],
        out_specs=[pl.BlockSpec((8, tm), lambda i: (0, i)), pl.BlockSpec((N_EXPERTS, LANES), lambda i: (0, 0))],
        out_shape=[jax.ShapeDtypeStruct((8, T), F32), jax.ShapeDtypeStruct((N_EXPERTS, LANES), F32)],
        scratch_shapes=[pltpu.VMEM((N_EXPERTS, LANES), F32)],
        compiler_params=_cparams(("arbitrary",)),
        name="rank",
    )(info_t, tri)


def _row_copy(src_hbm, row, dst, slot, sem):
    return pltpu.make_async_copy(src_hbm.at[pl.ds(row, 1), :], dst.at[pl.ds(slot, 1), :], sem)


def _expert_kernel(be_ref, nu_ref, src_ref, x_hbm, w1_ref, w3_ref, w2_ref, y_ref, xbuf, sem):
    j = pl.program_id(0)
    tm = xbuf.shape[0]

    @pl.when(j < nu_ref[0])
    def _():
        def issue(r, c):
            _row_copy(x_hbm, src_ref[0, 0, r], xbuf, r, sem).start()
            return c
        lax.fori_loop(0, tm, issue, 0, unroll=8)

        def drain(r, c):
            _row_copy(x_hbm, 0, xbuf, r, sem).wait()
            return c
        lax.fori_loop(0, tm, drain, 0, unroll=8)
        x = xbuf[...].astype(BF16)
        a = _dot(x, w1_ref[0])
        h = (a * _sigmoid(a)) * _dot(x, w3_ref[0])
        y_ref[...] = _dot(h.astype(BF16), w2_ref[0])

    @pl.when(j >= nu_ref[0])
    def _():
        y_ref[...] = jnp.zeros_like(y_ref)


def _experts(block_expert, n_used, src_tok, x1, w1, w3, w2):
    n_blk = block_expert.shape[0]
    tm = MOE_TM
    grid_spec = pltpu.PrefetchScalarGridSpec(
        num_scalar_prefetch=2,
        grid=(n_blk,),
        in_specs=[pl.BlockSpec((1, 1, tm), lambda j, be, nu: (j, 0, 0), memory_space=pltpu.SMEM),
                  pl.BlockSpec(memory_space=pl.ANY),
                  pl.BlockSpec((1, D_MODEL, D_EXPERT), lambda j, be, nu: (be[j], 0, 0)),
                  pl.BlockSpec((1, D_MODEL, D_EXPERT), lambda j, be, nu: (be[j], 0, 0)),
                  pl.BlockSpec((1, D_EXPERT, D_MODEL), lambda j, be, nu: (be[j], 0, 0))],
        out_specs=pl.BlockSpec((tm, D_MODEL), lambda j, be, nu: (j, 0)),
        scratch_shapes=[pltpu.VMEM((tm, D_MODEL), F32), pltpu.SemaphoreType.DMA(())],
    )
    return pl.pallas_call(
        _expert_kernel,
        grid_spec=grid_spec,
        out_shape=jax.ShapeDtypeStruct((n_blk * tm, D_MODEL), F32),
        compiler_params=_cparams(("arbitrary",)),
        name="experts",
    )(block_expert, n_used, src_tok.reshape(n_blk, 1, tm), x1, w1, w3, w2)


def _combine_kernel(dest_ref, y_hbm, info_ref, x_ref, g_ref, b_ref, o_ref, ybuf, sem):
    tm = x_ref.shape[0]

    def issue(r, c):
        _row_copy(y_hbm, dest_ref[0, 0, r], ybuf, r, sem).start()
        return c
    lax.fori_loop(0, 2 * tm, issue, 0, unroll=8)

    def drain(r, c):
        _row_copy(y_hbm, 0, ybuf, r, sem).wait()
        return c
    lax.fori_loop(0, 2 * tm, drain, 0, unroll=8)
    info = info_ref[...]
    moe = info[:, 2:3] * ybuf[0:tm, :] + info[:, 3:4] * ybuf[tm:2 * tm, :]
    o_ref[...] = _layer_norm(ALPHA * x_ref[...] + moe, g_ref[...], b_ref[...])


def _combine(dest, y_pad, info, x1, g, b):
    T = x1.shape[0]
    tm = COMB_TM
    cst = lambda a: pl.BlockSpec(a.shape, lambda i: (0,) * a.ndim)
    return pl.pallas_call(
        _combine_kernel,
        grid=(T // tm,),
        in_specs=[pl.BlockSpec((1, 1, 2 * tm), lambda i: (i, 0, 0), memory_space=pltpu.SMEM),
                  pl.BlockSpec(memory_space=pl.ANY),
                  pl.BlockSpec((tm, LANES), lambda i: (i, 0)),
                  pl.BlockSpec((tm, D_MODEL), lambda i: (i, 0)), cst(g), cst(b)],
        out_specs=pl.BlockSpec((tm, D_MODEL), lambda i: (i, 0)),
        out_shape=jax.ShapeDtypeStruct((T, D_MODEL), F32),
        scratch_shapes=[pltpu.VMEM((2 * tm, D_MODEL), F32), pltpu.SemaphoreType.DMA(())],
        compiler_params=_cparams(("arbitrary",)),
        name="combine",
    )(dest, y_pad, info, x1, g, b)


def _moe(x1, info, info_t, prm, l):
    T = x1.shape[0]
    tm = MOE_TM
    n_blk = (2 * T) // tm + N_EXPERTS
    ranks, cnt = _rank(info_t, prm["tri_rank"])
    counts = cnt[:, 0].astype(jnp.int32)
    padded = ((counts + tm - 1) // tm) * tm
    pad_end = jnp.cumsum(padded)
    pad_off = pad_end - padded
    eid = info_t[0:2].astype(jnp.int32)
    dest = pad_off[eid] + ranks[0:2].astype(jnp.int32)
    tok = jnp.broadcast_to(jnp.arange(T, dtype=jnp.int32), (2, T))
    src_tok = jnp.zeros((n_blk * tm,), jnp.int32).at[dest.reshape(-1)].set(tok.reshape(-1))
    block_expert = jnp.minimum(
        jnp.searchsorted(pad_end, jnp.arange(n_blk, dtype=jnp.int32) * tm, side="right"),
        N_EXPERTS - 1).astype(jnp.int32)
    n_used = (pad_end[-1] // tm).astype(jnp.int32).reshape(1)
    block_expert = jnp.where(jnp.arange(n_blk) < n_used[0], block_expert,
                             block_expert[jnp.maximum(n_used[0] - 1, 0)])
    y_pad = _experts(block_expert, n_used, src_tok, x1, prm["moe_w1"][l], prm["moe_w3"][l], prm["moe_w2"][l])
    ct = COMB_TM
    dest_blk = dest.reshape(2, T // ct, ct).transpose(1, 0, 2).reshape(T // ct, 1, 2 * ct)
    return _combine(dest_blk, y_pad, info, x1, prm["ln2_g"][l], prm["ln2_b"][l])


def _prep_params(p):
    L = p["w_in"].shape[0]
    w = p["w_in"]
    fg = jnp.pad(w[:, :, 1536:1544], ((0, 0), (0, 0), (0, LANES - FOX_HEADS)))
    w_in = jnp.concatenate([w[:, :, 0:1536], fg, w[:, :, 1544:2568], w[:, :, 2568:3592]], axis=-1)
    out = {
        "w_in": w_in.astype(BF16),
        "fbias": jnp.pad(p["fox_fgate_bias"], ((0, 0), (0, LANES - FOX_HEADS)))[:, None, :],
        "tri_proj": jnp.tril(jnp.ones((PROJ_TM, PROJ_TM), F32)),
    }
    vec = jnp.stack([p["rwkv_w0"], p["rwkv_a0"], p["rwkv_k_k"], p["rwkv_k_a"],
                     p["rwkv_r_k"].reshape(L, RWKV_W), p["rwkv_ln_g"], p["rwkv_ln_b"],
                     jnp.zeros((L, RWKV_W), F32)], axis=1)
    ti = np.arange(RWKV_TS)
    lt = ((ti[:, None] // RWKV_C == ti[None, :] // RWKV_C) & (ti[None, :] <= ti[:, None])).astype(np.float32)
    hi = np.arange(RWKV_W) // HEAD_DIM
    out.update({
        "rwkv_mu": jnp.concatenate([p["rwkv_mu_rkv"].reshape(L, 3 * RWKV_W), p["rwkv_mu_lora"]], axis=-1)[:, None, :],
        "rwkv_vec": vec,
        "rwkv_w2": p["rwkv_w2"], "rwkv_a2": p["rwkv_a2"], "rwkv_g2": p["rwkv_g2"],
        "rwkv_lt": jnp.asarray(lt),
        "block_ones": jnp.asarray((hi[:, None] == hi[None, :]).astype(np.float32)),
    })
    out.update(_ret_tables(p["x"].shape[1]))
    out["ret_gn"] = p["ret_gn_g"][:, None, :]
    pad_r = LANES - N_GROUPS - N_EXPERTS
    ri = np.arange(RANK_TM)
    out.update({
        "w_out": p["w_out"].astype(BF16),
        "ln1_g": p["ln1_g"][:, None, :], "ln1_b": p["ln1_b"][:, None, :],
        "ln2_g": p["ln2_g"][:, None, :], "ln2_b": p["ln2_b"][:, None, :],
        "w_router": jnp.pad(jnp.concatenate([p["moe_w_group"], p["moe_w_expert"]], axis=-1),
                            ((0, 0), (0, 0), (0, pad_r))),
        "b_router": jnp.pad(jnp.concatenate([p["moe_b_group"], p["moe_b_expert"]], axis=-1),
                            ((0, 0), (0, pad_r)))[:, None, :],
        "tri_rank": jnp.asarray((ri[:, None] < ri[None, :]).astype(np.float32), dtype=BF16),
        "moe_w1": p["moe_w1"].astype(BF16), "moe_w3": p["moe_w3"].astype(BF16),
        "moe_w2": p["moe_w2"].astype(BF16),
    })
    return out


def kernel(x, w_in, fox_fgate_bias, rwkv_mu_rkv, rwkv_mu_lora, rwkv_w0, rwkv_w2, rwkv_a0, rwkv_a2, rwkv_g2,
           rwkv_k_k, rwkv_k_a, rwkv_r_k, rwkv_ln_g, rwkv_ln_b, ret_gn_g, w_out, ln1_g, ln1_b, ln2_g, ln2_b,
           moe_w_group, moe_b_group, moe_w_expert, moe_b_expert, moe_w1, moe_w3, moe_w2):
    prm = _prep_params(dict(
        x=x, w_in=w_in, fox_fgate_bias=fox_fgate_bias, rwkv_mu_rkv=rwkv_mu_rkv, rwkv_mu_lora=rwkv_mu_lora,
        rwkv_w0=rwkv_w0, rwkv_w2=rwkv_w2, rwkv_a0=rwkv_a0, rwkv_a2=rwkv_a2, rwkv_g2=rwkv_g2,
        rwkv_k_k=rwkv_k_k, rwkv_k_a=rwkv_k_a, rwkv_r_k=rwkv_r_k, rwkv_ln_g=rwkv_ln_g, rwkv_ln_b=rwkv_ln_b,
        ret_gn_g=ret_gn_g, w_out=w_out, ln1_g=ln1_g, ln1_b=ln1_b, ln2_g=ln2_g, ln2_b=ln2_b,
        moe_w_group=moe_w_group, moe_b_group=moe_b_group, moe_w_expert=moe_w_expert,
        moe_b_expert=moe_b_expert, moe_w1=moe_w1, moe_w3=moe_w3, moe_w2=moe_w2))
    B, S, D = x.shape
    T = B * S
    for l in range(w_in.shape[0]):
        qa, ka, va, rw, rt = _project(x, prm["w_in"][l], prm["fbias"][l], prm["tri_proj"])
        y_fox = _fox_attention(qa, ka, va)
        y_rwkv = _rwkv(rw, prm, l)
        y_ret = _retention(rt, prm, l)
        x1, info, info_t = _out_proj(y_fox.reshape(T, FOX_W), y_rwkv.reshape(T, RWKV_W),
                                     y_ret.reshape(T, RET_W), x.reshape(T, D), prm, l)
        x = _moe(x1, info, info_t, prm, l).reshape(B, S, D)
    return x
```

```python
import functools
import math

import jax
import jax.numpy as jnp
import numpy as np
from jax import lax
from jax.experimental import pallas as pl
from jax.experimental.pallas import tpu as pltpu

F32 = jnp.float32
BF16 = jnp.bfloat16
HIGHEST = lax.Precision.HIGHEST

D_MODEL = 1024
DEPTH = 4
HEAD_DIM = 64
FOX_W, RWKV_W, RET_W = 512, 256, 256
FOX_HEADS, RWKV_HEADS, RET_HEADS = 8, 4, 4
W_LORA, A_LORA, G_LORA = 64, 64, 128
ROPE_BASE = 10000.0
N_GROUPS, EXPERTS_PER_GROUP, N_EXPERTS, D_EXPERT = 4, 8, 32, 512
ALPHA = (2.0 * DEPTH) ** 0.25
LN_EPS = 1e-5
LOG2E = math.log2(math.e)
RWKV_GN_EPS = 64e-5
RET_GN_EPS = 1e-6

LANES = 128
VMEM_LIMIT = 56 * 1024 * 1024

PROJ_TM = 512
ATT_T = 512
ATT_HEADS = 8
RWKV_TS = 512
RWKV_C = 64
RET_C = 256
OUT_TM = 512
RANK_TM = 512
MOE_TM = 256
COMB_TM = 256

P_FOX = 0
P_FG = 1536
P_RWKV = 1664
P_RET = 2688
P_TOT = 3712


def _cparams(sem):
    return pltpu.CompilerParams(dimension_semantics=sem, vmem_limit_bytes=VMEM_LIMIT)


def _dot(a, b, precision=None):
    return jnp.dot(a, b, preferred_element_type=F32, precision=precision)


def _dot_nt(a, b, precision=None):
    return lax.dot_general(a, b, (((1,), (1,)), ((), ())), preferred_element_type=F32,
                           precision=precision)


def _split(a, n):
    parts, r = [], a
    for i in range(n):
        h = r.astype(BF16)
        parts.append(h)
        if i + 1 < n:
            r = r - h.astype(F32)
    return parts


def _dot3(a, b, nt=False):
    d = _dot_nt if nt else _dot
    a1, a2 = _split(a, 2)
    b1, b2 = _split(b, 2)
    return (d(a1, b2) + d(a2, b1)) + d(a1, b1)


def _dot_01(a, m, n=3, left=False):
    d = (lambda p: _dot(m, p)) if left else (lambda p: _dot(p, m))
    parts = _split(a, n)
    acc = d(parts[-1])
    for p in parts[-2::-1]:
        acc = acc + d(p)
    return acc


def _log_sigmoid(z):
    return -(jnp.maximum(-z, 0.0) + jnp.log(1.0 + jnp.exp(-jnp.abs(z))))


def _sigmoid(z):
    return 1.0 / (1.0 + jnp.exp(-z))


def _iota(shape, dim):
    return lax.broadcasted_iota(jnp.int32, shape, dim)


def _proj_kernel(x_ref, w_ref, fb_ref, tri_ref, qa_ref, ka_ref, va_ref, rw_ref, rt_ref, carry_ref):
    tm = x_ref.shape[1]

    @pl.when(pl.program_id(1) == 0)
    def _():
        carry_ref[...] = jnp.zeros_like(carry_ref)

    xb = x_ref[0].astype(BF16)
    p_fox = _dot(xb, w_ref[:, P_FOX:P_FG])
    z = _dot(xb, w_ref[:, P_FG:P_RWKV]) + fb_ref[...]
    rw_ref[0] = _dot(xb, w_ref[:, P_RWKV:P_RET])
    rt_ref[0] = _dot(xb, w_ref[:, P_RET:P_TOT])

    c = _dot_01(_log_sigmoid(z), tri_ref[...], 3, left=True) + carry_ref[0:1, :]
    carry_ref[...] = jnp.broadcast_to(c[tm - 1:tm, :], carry_ref.shape)

    lane = _iota((tm, LANES), 1)
    for h in range(FOX_HEADS):
        cb = jnp.broadcast_to(c[:, h:h + 1], (tm, LANES)) * LOG2E
        hi = cb.astype(BF16).astype(F32)
        r1 = cb - hi
        mid = r1.astype(BF16).astype(F32)
        lo = r1 - mid
        tile = (h // 2) * LANES

        def head_lanes(base):
            t = p_fox[:, base + tile: base + tile + LANES]
            return pltpu.roll(t, 64, 1) if h % 2 else t

        q = head_lanes(0) * (LOG2E * HEAD_DIM ** -0.5)
        k = head_lanes(FOX_W)
        v = head_lanes(2 * FOX_W)
        qa = jnp.where(lane < 64, q, jnp.where(lane == 64, hi, jnp.where(lane == 65, mid,
             jnp.where(lane == 66, lo, jnp.where(lane < 70, 1.0, 0.0)))))
        ka = jnp.where(lane < 64, k, jnp.where(lane < 67, 1.0, jnp.where(lane == 67, -hi,
             jnp.where(lane == 68, -mid, jnp.where(lane == 69, -lo, 0.0)))))
        va = jnp.where(lane < 64, v, jnp.where(lane == 64, 1.0, 0.0))
        qa_ref[0, h] = qa.astype(BF16)
        ka_ref[0, h] = ka.astype(BF16)
        va_ref[0, h] = va.astype(BF16)


def _project(x, w_bf16, fbias, tri):
    B, S, D = x.shape
    tm = PROJ_TM
    aug = jax.ShapeDtypeStruct((B, FOX_HEADS, S, LANES), BF16)
    wide = jax.ShapeDtypeStruct((B, S, 1024), F32)
    aug_spec = pl.BlockSpec((1, FOX_HEADS, tm, LANES), lambda b, s: (b, 0, s, 0))
    wide_spec = pl.BlockSpec((1, tm, 1024), lambda b, s: (b, s, 0))
    return pl.pallas_call(
        _proj_kernel,
        grid=(B, S // tm),
        in_specs=[pl.BlockSpec((1, tm, D), lambda b, s: (b, s, 0)),
                  pl.BlockSpec((D, P_TOT), lambda b, s: (0, 0)),
                  pl.BlockSpec((1, LANES), lambda b, s: (0, 0)),
                  pl.BlockSpec((tm, tm), lambda b, s: (0, 0))],
        out_specs=[aug_spec, aug_spec, aug_spec, wide_spec, wide_spec],
        out_shape=[aug, aug, aug, wide, wide],
        scratch_shapes=[pltpu.VMEM((8, LANES), F32)],
        compiler_params=_cparams(("arbitrary", "arbitrary")),
        name="proj",
    )(x, w_bf16, fbias, tri)


def _fox_kernel(qi_ref, ki_ref, q_ref, k_ref, v_ref, o_ref, m_ref, acc_ref):
    p = pl.program_id(2)
    qi = qi_ref[p]
    ki = ki_ref[p]
    nh, t = q_ref.shape[1], q_ref.shape[2]

    @pl.when(ki == 0)
    def _():
        m_ref[...] = jnp.full_like(m_ref, -jnp.inf)
        acc_ref[...] = jnp.zeros_like(acc_ref)

    def step(masked):
        qk = lambda j: _dot_nt(q_ref[0, j], k_ref[0, j])
        ss = {j: qk(j) for j in range(min(2, nh))}
        for j in range(nh):
            if j + 2 < nh:
                ss[j + 2] = qk(j + 2)
            s = ss.pop(j)
            if masked:
                s = jnp.where(_iota((t, t), 1) <= _iota((t, t), 0), s, -jnp.inf)
            m_prev = m_ref[j]
            m_new = jnp.maximum(m_prev, jnp.max(s, axis=1, keepdims=True))
            alpha = jnp.exp2(m_prev - m_new)
            pm = jnp.exp2(s - jnp.concatenate([m_new] * (t // LANES), axis=1))
            acc_ref[j] = alpha * acc_ref[j] + _dot(pm.astype(BF16), v_ref[0, j])
            m_ref[j] = m_new

    @pl.when(ki < qi)
    def _():
        step(False)

    @pl.when(ki == qi)
    def _():
        step(True)
        lane = _iota((t, LANES), 1)
        for jp in range(nh // 2):
            a0, a1 = acc_ref[2 * jp], acc_ref[2 * jp + 1]
            o0, o1 = a0 / a0[:, 64:65], a1 / a1[:, 64:65]
            o_ref[0, :, jp * LANES:(jp + 1) * LANES] = jnp.where(
                lane < 64, o0, pltpu.roll(o1, 64, 1)).astype(o_ref.dtype)


def _fox_attention(qa, ka, va):
    B, H, S, _ = qa.shape
    t = ATT_T
    nh = ATT_HEADS
    nq = S // t
    qi_arr = np.concatenate([np.full(i + 1, i, np.int32) for i in range(nq)])
    ki_arr = np.concatenate([np.arange(i + 1, dtype=np.int32) for i in range(nq)])
    n_pairs = qi_arr.shape[0]
    grid_spec = pltpu.PrefetchScalarGridSpec(
        num_scalar_prefetch=2,
        grid=(B, H // nh, n_pairs),
        in_specs=[pl.BlockSpec((1, nh, t, LANES), lambda b, hp, p, qi, ki: (b, hp, qi[p], 0)),
                  pl.BlockSpec((1, nh, t, LANES), lambda b, hp, p, qi, ki: (b, hp, ki[p], 0)),
                  pl.BlockSpec((1, nh, t, LANES), lambda b, hp, p, qi, ki: (b, hp, ki[p], 0))],
        out_specs=pl.BlockSpec((1, t, nh * HEAD_DIM), lambda b, hp, p, qi, ki: (b, qi[p], hp)),
        scratch_shapes=[pltpu.VMEM((nh, t, LANES), F32), pltpu.VMEM((nh, t, LANES), F32)],
    )
    return pl.pallas_call(
        _fox_kernel,
        grid_spec=grid_spec,
        out_shape=jax.ShapeDtypeStruct((B, S, FOX_W), BF16),
        compiler_params=_cparams(("arbitrary", "arbitrary", "arbitrary")),
        name="fox_attn",
    )(jnp.asarray(qi_arr), jnp.asarray(ki_arr), qa, ka, va)


_RV_W0, _RV_A0, _RV_KK, _RV_KA, _RV_RK, _RV_LNG, _RV_LNB = range(7)


def _head_stack(x, n_heads):
    lane = _iota(x.shape, 1)
    return jnp.concatenate(
        [jnp.where(lane // HEAD_DIM == h, x, 0.0) for h in range(n_heads)], axis=0)


def _rwkv_kernel(p_ref, mu_ref, vec_ref, w2_ref, a2_ref, g2_ref, lt_ref, bo_ref, o_ref,
                 prev_ref, s_ref, at_ref, rt_ref, bt_ref, kt_ref, v_ref, wl_ref, y_ref, bon_ref, g_ref,
                 prb_ref, base_ref, pab_ref):
    nb, ts, _ = p_ref.shape
    C = RWKV_C
    H = RWKV_HEADS
    W = RWKV_W

    @pl.when(pl.program_id(0) == 0)
    def _():
        prev_ref[...] = jnp.zeros_like(prev_ref)
        s_ref[...] = jnp.zeros_like(s_ref)

    vec = vec_ref[...]
    row = lambda i: vec[i:i + 1, :]
    bo = bo_ref[...]

    for b in range(nb):
        p = p_ref[b]
        rowi = _iota(p.shape, 0)
        prev = jnp.where(rowi == 0, jnp.broadcast_to(prev_ref[b, 0:1, :], p.shape), pltpu.roll(p, 1, 0))
        prev_ref[b] = jnp.broadcast_to(p[ts - 1:ts, :], prev_ref.shape[1:])
        xm = p + (prev - p) * mu_ref[...]
        r, k, v = xm[:, 0:W], xm[:, W:2 * W], xm[:, 2 * W:3 * W]
        wd = xm[:, 3 * W:3 * W + W_LORA]
        ad = xm[:, 3 * W + W_LORA:3 * W + W_LORA + A_LORA]
        gd = xm[:, 3 * W + W_LORA + A_LORA:]
        lw = row(_RV_W0) + _dot3(jnp.tanh(wd), w2_ref[...])
        logdec = -jnp.exp(_log_sigmoid(lw) - 0.5)
        a = _sigmoid(row(_RV_A0) + _dot3(ad, a2_ref[...]))
        g_ref[b] = _dot3(_sigmoid(gd), g2_ref[...])
        kk = k * row(_RV_KK)
        kk = kk / jnp.maximum(jnp.sqrt(_dot_01(kk * kk, bo, 2)), 1e-12)
        k2 = k * (1.0 + (a - 1.0) * row(_RV_KA))
        cum = _dot_01(logdec, lt_ref[...], 3, left=True)
        wincl = jnp.exp(cum)
        winv = jnp.exp(-cum)
        at_ref[b] = -kk * jnp.exp(cum - logdec)
        rt_ref[b] = r * wincl
        bt_ref[b] = kk * a * winv
        kt_ref[b] = k2 * winv
        v_ref[b] = v
        wl_ref[b] = wincl
        bon_ref[b] = r * k2 * row(_RV_RK)

    li = _iota((C, H * C), 1) % C
    ri = _iota((C, H * C), 0)
    strict = li < ri
    incl = li <= ri
    bd = (_iota((W, W), 0) // HEAD_DIM) == (_iota((W, W), 1) // HEAD_DIM)

    def prep_body(ci, carry):
        c0 = pl.multiple_of(ci * C, C)
        sl = pl.ds(c0, C)
        for b in range(nb):
            at, rt, bt, kt, vv = at_ref[b, sl, :], rt_ref[b, sl, :], bt_ref[b, sl, :], kt_ref[b, sl, :], v_ref[b, sl, :]
            bstack, kstack, vstack = _head_stack(bt, H), _head_stack(kt, H), _head_stack(vv, H)
            pab_ref[b, sl, :] = jnp.where(strict, _dot3(at, bstack, nt=True), 0.0)
            pak = jnp.where(strict, _dot3(at, kstack, nt=True), 0.0)
            prb_ref[b, sl, :] = jnp.where(incl, _dot3(rt, bstack, nt=True), 0.0)
            prk = jnp.where(incl, _dot3(rt, kstack, nt=True), 0.0)
            base_ref[b, sl, :] = _dot3(pak, vstack)
            y_ref[b, sl, :] = _dot3(prk, vstack)
        return carry

    lax.fori_loop(0, ts // C, prep_body, 0)

    def chunk_body(ci, carry):
        c0 = pl.multiple_of(ci * C, C)
        sl = pl.ds(c0, C)
        s0 = [s_ref[b] for b in range(nb)]
        ar = [_dot3(jnp.concatenate([at_ref[b, sl, :], rt_ref[b, sl, :]], axis=0), s0[b], nt=True)
              for b in range(nb)]
        half = C // 2
        lo_lanes = _iota((8, LANES), 1) < HEAD_DIM
        lo_half = _iota((half, LANES), 1) < HEAD_DIM
        u = {}
        for b in range(nb):
            base = base_ref[b, sl, :] + ar[b][0:C]
            for hp in range(H // 2):
                u[b, hp] = [base[8 * j:8 * j + 8, hp * LANES:(hp + 1) * LANES] for j in range(C // 8)]
        pab = {(b, hp): pab_ref[b, sl, hp * LANES:(hp + 1) * LANES] for b in range(nb) for hp in range(H // 2)}

        def solve_half(t0):
            for t in range(t0, t0 + half - 1):
                for key, w in u.items():
                    u_t = jnp.broadcast_to(w[t // 8][t % 8:t % 8 + 1, :], (8, LANES))
                    for j in range((t + 1) // 8, (t0 + half) // 8):
                        rows = pab[key][8 * j:8 * j + 8, :]
                        coef = jnp.where(lo_lanes, jnp.broadcast_to(rows[:, t:t + 1], (8, LANES)),
                                         jnp.broadcast_to(rows[:, HEAD_DIM + t:HEAD_DIM + t + 1], (8, LANES)))
                        w[j] = w[j] + coef * u_t

        solve_half(0)
        for key, w in u.items():
            first = jnp.concatenate(w[0:half // 8], axis=0)
            a_lo = pab[key][half:C, 0:half]
            a_hi = pab[key][half:C, HEAD_DIM:HEAD_DIM + half]
            inc = jnp.where(lo_half, _dot3(a_lo, first), _dot3(a_hi, first))
            for j in range(half // 8, C // 8):
                w[j] = w[j] + inc[8 * j - half:8 * j - half + 8, :]
        solve_half(half)
        for b in range(nb):
            u_full = jnp.concatenate([jnp.concatenate(u[b, hp], axis=0) for hp in range(H // 2)], axis=1)
            y_ref[b, sl, :] = (y_ref[b, sl, :] + ar[b][C:2 * C]
                               + _dot3(prb_ref[b, sl, :], _head_stack(u_full, H)))
            upd = _dot3(jnp.concatenate([u_full, v_ref[b, sl, :]], axis=0).T,
                        jnp.concatenate([bt_ref[b, sl, :], kt_ref[b, sl, :]], axis=0))
            wl = wl_ref[b, pl.ds(c0 + C - 1, 1), :]
            s_ref[b] = (s0[b] + jnp.where(bd, upd, 0.0)) * wl
        return carry

    lax.fori_loop(0, ts // C, chunk_body, 0)

    for b in range(nb):
        y = y_ref[b]
        mean = _dot_01(y, bo, 2) * (1.0 / HEAD_DIM)
        d = y - mean
        var = _dot_01(d * d, bo, 2) * (1.0 / HEAD_DIM)
        yn = d * lax.rsqrt(var + RWKV_GN_EPS) * row(_RV_LNG) + row(_RV_LNB)
        bonus = _dot_01(bon_ref[b], bo, 2) * v_ref[b]
        o_ref[b] = (yn + bonus) * g_ref[b]


def _rwkv(rw, prm, l):
    B, S, _ = rw.shape
    ts = RWKV_TS
    full = lambda a: pl.BlockSpec(a.shape, lambda s: (0,) * a.ndim)
    args = [prm["rwkv_mu"][l], prm["rwkv_vec"][l], prm["rwkv_w2"][l], prm["rwkv_a2"][l], prm["rwkv_g2"][l],
            prm["rwkv_lt"], prm["block_ones"]]
    big = lambda: pltpu.VMEM((B, ts, RWKV_W), F32)
    return pl.pallas_call(
        _rwkv_kernel,
        grid=(S // ts,),
        in_specs=[pl.BlockSpec((B, ts, 1024), lambda s: (0, s, 0))] + [full(a) for a in args],
        out_specs=pl.BlockSpec((B, ts, RWKV_W), lambda s: (0, s, 0)),
        out_shape=jax.ShapeDtypeStruct((B, S, RWKV_W), F32),
        scratch_shapes=[pltpu.VMEM((B, 8, 1024), F32), pltpu.VMEM((B, RWKV_W, RWKV_W), F32)]
                       + [big() for _ in range(12)],
        compiler_params=_cparams(("arbitrary",)),
        name="rwkv",
    )(rw, *args)


def _ret_kernel(p_ref, cos_ref, sin_ref, dm_ref, qd_ref, kd_ref, dmat_ref, bo_ref, gn_ref, o_ref, r_ref):
    C = p_ref.shape[1]
    W = RET_W
    H = RET_HEADS

    @pl.when(pl.program_id(1) == 0)
    def _():
        r_ref[...] = jnp.zeros_like(r_ref)

    p = p_ref[0]
    cos, sin = cos_ref[...], sin_ref[...]
    first_half = (_iota((C, W), 1) % HEAD_DIM) < (HEAD_DIM // 2)

    def rope(x):
        partner = jnp.where(first_half, pltpu.roll(x, W - HEAD_DIM // 2, 1), pltpu.roll(x, HEAD_DIM // 2, 1))
        return x * cos + partner * sin

    q = rope(p[:, 0:W])
    k = rope(p[:, W:2 * W]) * (HEAD_DIM ** -0.5)
    v = p[:, 2 * W:3 * W]
    g = p[:, 3 * W:4 * W]
    state = r_ref[...]
    scores = _dot_nt(q, _head_stack(k, H)) * dm_ref[...]
    y = _dot(scores, _head_stack(v, H)) + _dot(q * qd_ref[...], state)
    kv = _dot((k * kd_ref[...]).T, v)
    bd = (_iota((W, W), 0) // HEAD_DIM) == (_iota((W, W), 1) // HEAD_DIM)
    r_ref[...] = state * dmat_ref[...] + jnp.where(bd, kv, 0.0)

    bo = bo_ref[...]
    mean = _dot_01(y, bo, 2) * (1.0 / HEAD_DIM)
    d = y - mean
    var = _dot_01(d * d, bo, 2) * (1.0 / HEAD_DIM)
    o_ref[0] =d * lax.rsqrt(var + RET_GN_EPS) * gn_ref[...] * (g * _sigmoid(g))


def _retention(rt, prm, l):
    B, S, _ = rt.shape
    C = RET_C
    cst = lambda a: pl.BlockSpec(a.shape, lambda b, s: (0,) * a.ndim)
    consts = [prm["ret_dmask"], prm["ret_qdec"], prm["ret_kdec"], prm["ret_dmat"], prm["block_ones"],
              prm["ret_gn"][l]]
    return pl.pallas_call(
        _ret_kernel,
        grid=(B, S // C),
        in_specs=[pl.BlockSpec((1, C, 1024), lambda b, s: (b, s, 0)),
                  pl.BlockSpec((C, RET_W), lambda b, s: (s, 0)),
                  pl.BlockSpec((C, RET_W), lambda b, s: (s, 0))] + [cst(a) for a in consts],
        out_specs=pl.BlockSpec((1, C, RET_W), lambda b, s: (b, s, 0)),
        out_shape=jax.ShapeDtypeStruct((B, S, RET_W), F32),
        scratch_shapes=[pltpu.VMEM((RET_W, RET_W), F32)],
        compiler_params=_cparams(("arbitrary", "arbitrary")),
        name="retention",
    )(rt, prm["ret_cos"], prm["ret_sin"], *consts)


def _ret_tables(S):
    C, H, d = RET_C, RET_HEADS, HEAD_DIM
    half = d // 2
    inv = ROPE_BASE ** (-jnp.arange(half, dtype=F32) / half)
    ang = jnp.arange(S, dtype=F32)[:, None] * inv[None, :]
    cos, sin = jnp.cos(ang), jnp.sin(ang)
    cos_t = jnp.tile(jnp.concatenate([cos, cos], axis=1), (1, H))
    sin_t = jnp.tile(jnp.concatenate([-sin, sin], axis=1), (1, H))
    log_gamma = jnp.log1p(-jnp.exp2(-5.0 - jnp.arange(H, dtype=F32)))
    idx = jnp.arange(C, dtype=F32)
    diff = idx[:, None] - idx[None, :]
    dmask = jnp.where(diff >= 0, jnp.exp(jnp.maximum(diff, 0.0) * log_gamma[:, None, None]), 0.0)
    dmask = jnp.transpose(dmask, (1, 0, 2)).reshape(C, H * C)
    rep = lambda a: jnp.repeat(a, d, axis=1)
    kdec = rep(jnp.exp((C - 1.0 - idx)[:, None] * log_gamma[None, :]))
    qdec = rep(jnp.exp((idx + 1.0)[:, None] * log_gamma[None, :]))
    hid = np.arange(H * d) // d
    bd = jnp.asarray(hid[:, None] == hid[None, :])
    dmat = jnp.where(bd, jnp.repeat(jnp.exp(C * log_gamma), d)[:, None], 0.0)
    return {"ret_cos": cos_t, "ret_sin": sin_t, "ret_dmask": dmask, "ret_qdec": qdec, "ret_kdec": kdec,
            "ret_dmat": dmat}


def _layer_norm(z, g, b):
    mu = jnp.mean(z, axis=1, keepdims=True)
    d = z - mu
    var = jnp.mean(d * d, axis=1, keepdims=True)
    return d * lax.rsqrt(var + LN_EPS) * g + b


ROW_TILES = D_MODEL // LANES


def _out_kernel(yf_ref, yr_ref, yt_ref, x_ref, w_ref, g_ref, b_ref, wr_ref, br_ref,
                x1_ref, info_ref, info_t_ref):
    tm = x_ref.shape[0]
    mixed = (_dot(yf_ref[...], w_ref[0:FOX_W, :])
             + _dot(yr_ref[...].astype(BF16), w_ref[FOX_W:FOX_W + RWKV_W, :])
             + _dot(yt_ref[...].astype(BF16), w_ref[FOX_W + RWKV_W:, :]))
    x1 = _layer_norm(ALPHA * x_ref[...] + mixed, g_ref[...], b_ref[...])
    x1_ref[...] = x1

    logits = _dot(x1, wr_ref[...], HIGHEST) + br_ref[...]
    lane = _iota((tm, LANES), 1).astype(F32)
    ninf = -jnp.inf
    first = lambda hit: jnp.min(jnp.where(hit, lane, float(LANES)), axis=1, keepdims=True)
    gl = jnp.where(lane < N_GROUPS, logits, ninf)
    gmax = jnp.max(gl, axis=1, keepdims=True)
    grp = first(gl == gmax)
    g_gate = 1.0 / jnp.sum(jnp.exp(gl - gmax), axis=1, keepdims=True)
    lo = N_GROUPS + grp * EXPERTS_PER_GROUP
    el = jnp.where(jnp.logical_and(lane >= lo, lane < lo + EXPERTS_PER_GROUP), logits, ninf)
    v1 = jnp.max(el, axis=1, keepdims=True)
    i1 = first(el == v1)
    el2 = jnp.where(lane == i1, ninf, el)
    v2 = jnp.max(el2, axis=1, keepdims=True)
    i2 = first(el2 == v2)
    e = jnp.exp(v2 - v1)
    den = 1.0 / (1.0 + e)
    info = jnp.where(lane == 0, i1 - N_GROUPS, jnp.where(lane == 1, i2 - N_GROUPS,
           jnp.where(lane == 2, g_gate * den, jnp.where(lane == 3, g_gate * e * den, 0.0))))
    info_ref[...] = info
    info_t_ref[...] = info.T[0:8, :]


def _out_proj(yf, yr, yt, x, prm, l):
    T = x.shape[0]
    tm = OUT_TM
    rows = lambda w: pl.BlockSpec((tm, w), lambda i: (i, 0))
    cst = lambda a: pl.BlockSpec(a.shape, lambda i: (0,) * a.ndim)
    consts = [prm["w_out"][l], prm["ln1_g"][l], prm["ln1_b"][l], prm["w_router"][l], prm["b_router"][l]]
    return pl.pallas_call(
        _out_kernel,
        grid=(T // tm,),
        in_specs=[rows(FOX_W), rows(RWKV_W), rows(RET_W), rows(D_MODEL)] + [cst(a) for a in consts],
        out_specs=[rows(D_MODEL), rows(LANES), pl.BlockSpec((8, tm), lambda i: (0, i))],
        out_shape=[jax.ShapeDtypeStruct((T, D_MODEL), F32), jax.ShapeDtypeStruct((T, LANES), F32),
                   jax.ShapeDtypeStruct((8, T), F32)],
        compiler_params=_cparams(("arbitrary",)),
        name="out_proj",
    )(yf, yr, yt, x, *consts)


def _rank_kernel(it_ref, tri_ref, rk_ref, cnt_ref, carry_ref):
    tm = it_ref.shape[1]

    @pl.when(pl.program_id(0) == 0)
    def _():
        carry_ref[...] = jnp.zeros_like(carry_ref)

    sub = _iota((N_EXPERTS, tm), 0).astype(F32)
    oh1 = sub == it_ref[0:1, :]
    oh2 = sub == it_ref[1:2, :]
    oh = jnp.where(jnp.logical_or(oh1, oh2), 1.0, 0.0)
    before = _dot(oh.astype(BF16), tri_ref[...]) + carry_ref[:, 0:1]
    r1 = jnp.sum(jnp.where(oh1, before, 0.0), axis=0, keepdims=True)
    r2 = jnp.sum(jnp.where(oh2, before, 0.0), axis=0, keepdims=True)
    rk_ref[...] = jnp.concatenate([r1, r2, jnp.zeros((6, tm), F32)], axis=0)
    carry_ref[...] = carry_ref[...] + jnp.sum(oh, axis=1, keepdims=True)
    cnt_ref[...] = carry_ref[...]


def _rank(info_t, tri):
    T = info_t.shape[1]
    tm = RANK_TM
    return pl.pallas_call(
        _rank_kernel,
        grid=(T // tm,),
        in_specs=[pl.BlockSpec((8, tm), lambda i: (0, i)), pl.BlockSpec((tm, tm), lambda i: (0, 0))],
        out_specs=[pl.BlockSpec((8, tm), lambda i: (0, i)), pl.BlockSpec((N_EXPERTS, LANES), lambda i: (0, 0))],
        out_shape=[jax.ShapeDtypeStruct((8, T), F32), jax.ShapeDtypeStruct((N_EXPERTS, LANES), F32)],
        scratch_shapes=[pltpu.VMEM((N_EXPERTS, LANES), F32)],
        compiler_params=_cparams(("arbitrary",)),
        name="rank",
    )(info_t, tri)


def _row_copy(src_hbm, row, dst, r, sem):
    return pltpu.make_async_copy(src_hbm.at[row], dst.at[:, r, :], sem)


def _expert_kernel(be_ref, nu_ref, src0_ref, src1_ref, x_hbm, w1_ref, w3_ref, w2_ref, y_hbm,
                   xbuf, ybuf, sem_in, sem_out):
    j = pl.program_id(0)
    tm = xbuf.shape[2]
    nu = nu_ref[0]
    slot = j % 2

    def gather(src_ref, s):
        for r in range(tm):
            _row_copy(x_hbm, src_ref[0, 0, r], xbuf.at[s], r, sem_in.at[s]).start()

    def drain_in(s):
        def body(r, c):
            _row_copy(x_hbm, 0, xbuf.at[s], r, sem_in.at[s]).wait()
            return c
        lax.fori_loop(0, tm, body, 0, unroll=8)

    def out_copy(s, c, blk):
        return pltpu.make_async_copy(ybuf.at[s, c], y_hbm.at[pl.ds(blk * tm, tm), c, :], sem_out.at[s])

    def drain_out(s):
        for c in range(ROW_TILES):
            out_copy(s, c, 0).wait()

    @pl.when(j == 0)
    def _():
        gather(src0_ref, 0)

    @pl.when(jnp.logical_and(j >= 2, j < nu))
    def _():
        drain_out(slot)

    @pl.when(j < nu)
    def _():
        drain_in(slot)
        x = jnp.concatenate([xbuf[slot, c] for c in range(ROW_TILES)], axis=1).astype(BF16)
        gather(src1_ref, 1 - slot)
        a = _dot(x, w1_ref[0])
        h = (a * _sigmoid(a)) * _dot(x, w3_ref[0])
        y = _dot(h.astype(BF16), w2_ref[0])
        for c in range(ROW_TILES):
            ybuf[slot, c] = y[:, c * LANES:(c + 1) * LANES]
        for c in range(ROW_TILES):
            out_copy(slot, c, j).start()

    @pl.when(j == nu)
    def _():
        drain_in(slot)
        drain_out(1 - slot)

    @pl.when(jnp.logical_and(j == nu, nu >= 2))
    def _():
        drain_out(slot)

    @pl.when(j == nu)
    def _():
        ybuf[0] = jnp.zeros(ybuf.shape[1:], ybuf.dtype)

    @pl.when(j >= nu)
    def _():
        for c in range(ROW_TILES):
            out_copy(0, c, j).start()
        drain_out(0)


def _experts(block_expert, n_used, src_tok, x1, w1, w3, w2):
    n_blk = block_expert.shape[0]
    tm = MOE_TM
    src_tok = src_tok.reshape(n_blk, 1, tm)
    grid_spec = pltpu.PrefetchScalarGridSpec(
        num_scalar_prefetch=2,
        grid=(n_blk,),
        in_specs=[pl.BlockSpec((1, 1, tm), lambda j, be, nu: (0, 0, 0), memory_space=pltpu.SMEM),
                  pl.BlockSpec((1, 1, tm), lambda j, be, nu: (jnp.minimum(j + 1, n_blk - 1), 0, 0),
                               memory_space=pltpu.SMEM),
                  pl.BlockSpec(memory_space=pl.ANY),
                  pl.BlockSpec((1, D_MODEL, D_EXPERT), lambda j, be, nu: (be[j], 0, 0)),
                  pl.BlockSpec((1, D_MODEL, D_EXPERT), lambda j, be, nu: (be[j], 0, 0)),
                  pl.BlockSpec((1, D_EXPERT, D_MODEL), lambda j, be, nu: (be[j], 0, 0))],
        out_specs=pl.BlockSpec(memory_space=pl.ANY),
        scratch_shapes=[pltpu.VMEM((2, ROW_TILES, tm, LANES), F32), pltpu.VMEM((2, ROW_TILES, tm, LANES), F32),
                        pltpu.SemaphoreType.DMA((2,)), pltpu.SemaphoreType.DMA((2,))],
    )
    return pl.pallas_call(
        _expert_kernel,
        grid_spec=grid_spec,
        out_shape=jax.ShapeDtypeStruct((n_blk * tm, ROW_TILES, LANES), F32),
        compiler_params=_cparams(("arbitrary",)),
        name="experts",
    )(block_expert, n_used, src_tok, src_tok, x1, w1, w3, w2)


def _combine_kernel(dest0_ref, dest1_ref, y_hbm, info_ref, x_ref, g_ref, b_ref, o_ref, ybuf, sem):
    i = pl.program_id(0)
    tm = x_ref.shape[0]
    slot = i % 2

    def gather(dest_ref, s):
        for r in range(2 * tm):
            _row_copy(y_hbm, dest_ref[0, 0, r], ybuf.at[s], r, sem.at[s]).start()

    @pl.when(i == 0)
    def _():
        gather(dest0_ref, 0)

    @pl.when(i + 1 < pl.num_programs(0))
    def _():
        gather(dest1_ref, 1 - slot)

    def drain(r, c):
        _row_copy(y_hbm, 0, ybuf.at[slot], r, sem.at[slot]).wait()
        return c
    lax.fori_loop(0, 2 * tm, drain, 0, unroll=8)
    info = info_ref[...]
    rows = lambda lo: jnp.concatenate([ybuf[slot, c, pl.ds(lo, tm), :] for c in range(ROW_TILES)], axis=1)
    moe = info[:, 2:3] * rows(0) + info[:, 3:4] * rows(tm)
    o_ref[...] = _layer_norm(ALPHA * x_ref[...] + moe, g_ref[...], b_ref[...])


def _combine(dest, y_pad, info, x1, g, b):
    T = x1.shape[0]
    tm = COMB_TM
    cst = lambda a: pl.BlockSpec(a.shape, lambda i: (0,) * a.ndim)
    n = T // tm
    return pl.pallas_call(
        _combine_kernel,
        grid=(n,),
        in_specs=[pl.BlockSpec((1, 1, 2 * tm), lambda i: (0, 0, 0), memory_space=pltpu.SMEM),
                  pl.BlockSpec((1, 1, 2 * tm), lambda i: (jnp.minimum(i + 1, n - 1), 0, 0),
                               memory_space=pltpu.SMEM),
                  pl.BlockSpec(memory_space=pl.ANY),
                  pl.BlockSpec((tm, LANES), lambda i: (i, 0)),
                  pl.BlockSpec((tm, D_MODEL), lambda i: (i, 0)), cst(g), cst(b)],
        out_specs=pl.BlockSpec((tm, D_MODEL), lambda i: (i, 0)),
        out_shape=jax.ShapeDtypeStruct((T, D_MODEL), F32),
        scratch_shapes=[pltpu.VMEM((2, ROW_TILES, 2 * tm, LANES), F32), pltpu.SemaphoreType.DMA((2,))],
        compiler_params=_cparams(("arbitrary",)),
        name="combine",
    )(dest, dest, y_pad, info, x1, g, b)


def _moe(x1, x1t, info, info_t, prm, l):
    T = x1.shape[0]
    tm = MOE_TM
    n_blk = (2 * T) // tm + N_EXPERTS
    ranks, cnt = _rank(info_t, prm["tri_rank"])
    counts = cnt[:, 0].astype(jnp.int32)
    padded = ((counts + tm - 1) // tm) * tm
    pad_end = jnp.cumsum(padded)
    pad_off = pad_end - padded
    eid = info_t[0:2].astype(jnp.int32)
    e_iota = jnp.arange(N_EXPERTS, dtype=jnp.int32)
    off = jnp.sum(jnp.where(eid[:, None, :] == e_iota[None, :, None], pad_off[None, :, None], 0), axis=1)
    dest = off + ranks[0:2].astype(jnp.int32)
    tok = jnp.broadcast_to(jnp.arange(T, dtype=jnp.int32), (2, T))
    src_tok = jnp.zeros((n_blk * tm,), jnp.int32).at[dest.reshape(-1)].set(tok.reshape(-1))
    blk_start = jnp.arange(n_blk, dtype=jnp.int32) * tm
    block_expert = jnp.minimum(jnp.sum((pad_end[None, :] <= blk_start[:, None]).astype(jnp.int32), axis=1),
                               N_EXPERTS - 1)
    n_used = (pad_end[-1] // tm).astype(jnp.int32).reshape(1)
    block_expert = jnp.where(jnp.arange(n_blk) < n_used[0], block_expert,
                             block_expert[jnp.maximum(n_used[0] - 1, 0)])
    y_pad = _experts(block_expert, n_used, src_tok, x1t, prm["moe_w1"][l], prm["moe_w3"][l], prm["moe_w2"][l])
    ct = COMB_TM
    dest_blk = dest.reshape(2, T // ct, ct).transpose(1, 0, 2).reshape(T // ct, 1, 2 * ct)
    return _combine(dest_blk, y_pad, info, x1, prm["ln2_g"][l], prm["ln2_b"][l])


def _prep_params(p):
    L = p["w_in"].shape[0]
    w = p["w_in"]
    fg = jnp.pad(w[:, :, 1536:1544], ((0, 0), (0, 0), (0, LANES - FOX_HEADS)))
    w_in = jnp.concatenate([w[:, :, 0:1536], fg, w[:, :, 1544:2568], w[:, :, 2568:3592]], axis=-1)
    out = {
        "w_in": w_in.astype(BF16),
        "fbias": jnp.pad(p["fox_fgate_bias"], ((0, 0), (0, LANES - FOX_HEADS)))[:, None, :],
        "tri_proj": jnp.tril(jnp.ones((PROJ_TM, PROJ_TM), BF16)),
    }
    vec = jnp.stack([p["rwkv_w0"], p["rwkv_a0"], p["rwkv_k_k"], p["rwkv_k_a"],
                     p["rwkv_r_k"].reshape(L, RWKV_W), p["rwkv_ln_g"], p["rwkv_ln_b"],
                     jnp.zeros((L, RWKV_W), F32)], axis=1)
    ti = np.arange(RWKV_TS)
    lt = ((ti[:, None] // RWKV_C == ti[None, :] // RWKV_C) & (ti[None, :] <= ti[:, None])).astype(np.float32)
    hi = np.arange(RWKV_W) // HEAD_DIM
    out.update({
        "rwkv_mu": jnp.concatenate([p["rwkv_mu_rkv"].reshape(L, 3 * RWKV_W), p["rwkv_mu_lora"]], axis=-1)[:, None, :],
        "rwkv_vec": vec,
        "rwkv_w2": p["rwkv_w2"], "rwkv_a2": p["rwkv_a2"], "rwkv_g2": p["rwkv_g2"],
        "rwkv_lt": jnp.asarray(lt, dtype=BF16),
        "block_ones": jnp.asarray((hi[:, None] == hi[None, :]).astype(np.float32), dtype=BF16),
    })
    out.update(_ret_tables(p["x"].shape[1]))
    out["ret_gn"] = p["ret_gn_g"][:, None, :]
    pad_r = LANES - N_GROUPS - N_EXPERTS
    ri = np.arange(RANK_TM)
    out.update({
        "w_out": p["w_out"].astype(BF16),
        "ln1_g": p["ln1_g"][:, None, :], "ln1_b": p["ln1_b"][:, None, :],
        "ln2_g": p["ln2_g"][:, None, :], "ln2_b": p["ln2_b"][:, None, :],
        "w_router": jnp.pad(jnp.concatenate([p["moe_w_group"], p["moe_w_expert"]], axis=-1),
                            ((0, 0), (0, 0), (0, pad_r))),
        "b_router": jnp.pad(jnp.concatenate([p["moe_b_group"], p["moe_b_expert"]], axis=-1),
                            ((0, 0), (0, pad_r)))[:, None, :],
        "tri_rank": jnp.asarray((ri[:, None] < ri[None, :]).astype(np.float32), dtype=BF16),
        "moe_w1": p["moe_w1"].astype(BF16), "moe_w3": p["moe_w3"].astype(BF16),
        "moe_w2": p["moe_w2"].astype(BF16),
    })
    return out


def kernel(x, w_in, fox_fgate_bias, rwkv_mu_rkv, rwkv_mu_lora, rwkv_w0, rwkv_w2, rwkv_a0, rwkv_a2, rwkv_g2,
           rwkv_k_k, rwkv_k_a, rwkv_r_k, rwkv_ln_g, rwkv_ln_b, ret_gn_g, w_out, ln1_g, ln1_b, ln2_g, ln2_b,
           moe_w_group, moe_b_group, moe_w_expert, moe_b_expert, moe_w1, moe_w3, moe_w2):
    prm = _prep_params(dict(
        x=x, w_in=w_in, fox_fgate_bias=fox_fgate_bias, rwkv_mu_rkv=rwkv_mu_rkv, rwkv_mu_lora=rwkv_mu_lora,
        rwkv_w0=rwkv_w0, rwkv_w2=rwkv_w2, rwkv_a0=rwkv_a0, rwkv_a2=rwkv_a2, rwkv_g2=rwkv_g2,
        rwkv_k_k=rwkv_k_k, rwkv_k_a=rwkv_k_a, rwkv_r_k=rwkv_r_k, rwkv_ln_g=rwkv_ln_g, rwkv_ln_b=rwkv_ln_b,
        ret_gn_g=ret_gn_g, w_out=w_out, ln1_g=ln1_g, ln1_b=ln1_b, ln2_g=ln2_g, ln2_b=ln2_b,
        moe_w_group=moe_w_group, moe_b_group=moe_b_group, moe_w_expert=moe_w_expert,
        moe_b_expert=moe_b_expert, moe_w1=moe_w1, moe_w3=moe_w3, moe_w2=moe_w2))
    B, S, D = x.shape
    T = B * S
    for l in range(w_in.shape[0]):
        qa, ka, va, rw, rt = _project(x, prm["w_in"][l], prm["fbias"][l], prm["tri_proj"])
        y_fox = _fox_attention(qa, ka, va)
        y_rwkv = _rwkv(rw, prm, l)
        y_ret = _retention(rt, prm, l)
        x1, info, info_t = _out_proj(y_fox.reshape(T, FOX_W), y_rwkv.reshape(T, RWKV_W),
                                     y_ret.reshape(T, RET_W), x.reshape(T, D), prm, l)
        x = _moe(x1, x1.reshape(T, ROW_TILES, LANES), info, info_t, prm, l).reshape(B, S, D)
    return x
```

```python
import functools
import math

import jax
import jax.numpy as jnp
import numpy as np
from jax import lax
from jax.experimental import pallas as pl
from jax.experimental.pallas import tpu as pltpu

F32 = jnp.float32
BF16 = jnp.bfloat16
HIGHEST = lax.Precision.HIGHEST

D_MODEL = 1024
DEPTH = 4
HEAD_DIM = 64
FOX_W, RWKV_W, RET_W = 512, 256, 256
FOX_HEADS, RWKV_HEADS, RET_HEADS = 8, 4, 4
W_LORA, A_LORA, G_LORA = 64, 64, 128
ROPE_BASE = 10000.0
N_GROUPS, EXPERTS_PER_GROUP, N_EXPERTS, D_EXPERT = 4, 8, 32, 512
ALPHA = (2.0 * DEPTH) ** 0.25
LN_EPS = 1e-5
LOG2E = math.log2(math.e)
RWKV_GN_EPS = 64e-5
RET_GN_EPS = 1e-6

LANES = 128
VMEM_LIMIT = 56 * 1024 * 1024

PROJ_TM = 512
ATT_T = 1024
ATT_HEADS = 8
RWKV_TS = 512
RWKV_C = 64
RET_C = 256
OUT_TM = 512
RANK_TM = 512
MOE_TM = 256
COMB_TM = 256

P_FOX = 0
P_FG = 1536
P_RWKV = 1664
P_RET = 2688
P_TOT = 3712


def _cparams(sem):
    return pltpu.CompilerParams(dimension_semantics=sem, vmem_limit_bytes=VMEM_LIMIT)


def _dot(a, b, precision=None):
    return jnp.dot(a, b, preferred_element_type=F32, precision=precision)


def _dot_nt(a, b, precision=None):
    return lax.dot_general(a, b, (((1,), (1,)), ((), ())), preferred_element_type=F32,
                           precision=precision)


def _split(a, n):
    parts, r = [], a
    for i in range(n):
        h = r.astype(BF16)
        parts.append(h)
        if i + 1 < n:
            r = r - h.astype(F32)
    return parts


def _dot3(a, b, nt=False):
    d = _dot_nt if nt else _dot
    a1, a2 = _split(a, 2)
    b1, b2 = _split(b, 2)
    return (d(a1, b2) + d(a2, b1)) + d(a1, b1)


def _dot_01(a, m, n=3, left=False):
    d = (lambda p: _dot(m, p)) if left else (lambda p: _dot(p, m))
    parts = _split(a, n)
    acc = d(parts[-1])
    for p in parts[-2::-1]:
        acc = acc + d(p)
    return acc


def _log_sigmoid(z):
    return -(jnp.maximum(-z, 0.0) + jnp.log(1.0 + jnp.exp(-jnp.abs(z))))


def _sigmoid(z):
    return 1.0 / (1.0 + jnp.exp(-z))


def _iota(shape, dim):
    return lax.broadcasted_iota(jnp.int32, shape, dim)


def _proj_kernel(x_ref, w_ref, fb_ref, tri_ref, qa_ref, ka_ref, va_ref, rw_ref, rt_ref, carry_ref):
    tm = x_ref.shape[1]

    @pl.when(pl.program_id(1) == 0)
    def _():
        carry_ref[...] = jnp.zeros_like(carry_ref)

    xb = x_ref[0].astype(BF16)
    p_fox = _dot(xb, w_ref[:, P_FOX:P_FG])
    z = _dot(xb, w_ref[:, P_FG:P_RWKV]) + fb_ref[...]
    rw_ref[0] = _dot(xb, w_ref[:, P_RWKV:P_RET])
    rt_ref[0] = _dot(xb, w_ref[:, P_RET:P_TOT])

    c = _dot_01(_log_sigmoid(z), tri_ref[...], 3, left=True) + carry_ref[0:1, :]
    carry_ref[...] = jnp.broadcast_to(c[tm - 1:tm, :], carry_ref.shape)

    lane = _iota((tm, LANES), 1)
    for h in range(FOX_HEADS):
        cb = jnp.broadcast_to(c[:, h:h + 1], (tm, LANES)) * LOG2E
        hi = cb.astype(BF16).astype(F32)
        r1 = cb - hi
        mid = r1.astype(BF16).astype(F32)
        lo = r1 - mid
        tile = (h // 2) * LANES

        def head_lanes(base):
            t = p_fox[:, base + tile: base + tile + LANES]
            return pltpu.roll(t, 64, 1) if h % 2 else t

        q = head_lanes(0) * (LOG2E * HEAD_DIM ** -0.5)
        k = head_lanes(FOX_W)
        v = head_lanes(2 * FOX_W)
        qa = jnp.where(lane < 64, q, jnp.where(lane == 64, hi, jnp.where(lane == 65, mid,
             jnp.where(lane == 66, lo, jnp.where(lane < 70, 1.0, 0.0)))))
        ka = jnp.where(lane < 64, k, jnp.where(lane < 67, 1.0, jnp.where(lane == 67, -hi,
             jnp.where(lane == 68, -mid, jnp.where(lane == 69, -lo, 0.0)))))
        va = jnp.where(lane < 64, v, jnp.where(lane == 64, 1.0, 0.0))
        qa_ref[0, h] = qa.astype(BF16)
        ka_ref[0, h] = ka.astype(BF16)
        va_ref[0, h] = va.astype(BF16)


def _project(x, w_bf16, fbias, tri):
    B, S, D = x.shape
    tm = PROJ_TM
    aug = jax.ShapeDtypeStruct((B, FOX_HEADS, S, LANES), BF16)
    wide = jax.ShapeDtypeStruct((B, S, 1024), F32)
    aug_spec = pl.BlockSpec((1, FOX_HEADS, tm, LANES), lambda b, s: (b, 0, s, 0))
    wide_spec = pl.BlockSpec((1, tm, 1024), lambda b, s: (b, s, 0))
    return pl.pallas_call(
        _proj_kernel,
        grid=(B, S // tm),
        in_specs=[pl.BlockSpec((1, tm, D), lambda b, s: (b, s, 0)),
                  pl.BlockSpec((D, P_TOT), lambda b, s: (0, 0)),
                  pl.BlockSpec((1, LANES), lambda b, s: (0, 0)),
                  pl.BlockSpec((tm, tm), lambda b, s: (0, 0))],
        out_specs=[aug_spec, aug_spec, aug_spec, wide_spec, wide_spec],
        out_shape=[aug, aug, aug, wide, wide],
        scratch_shapes=[pltpu.VMEM((8, LANES), F32)],
        compiler_params=_cparams(("arbitrary", "arbitrary")),
        name="proj",
    )(x, w_bf16, fbias, tri)


def _fox_kernel(qi_ref, ki_ref, q_ref, k_ref, v_ref, o_ref, m_ref, acc_ref):
    p = pl.program_id(2)
    qi = qi_ref[p]
    ki = ki_ref[p]
    nh, t = q_ref.shape[1], q_ref.shape[2]

    @pl.when(ki == 0)
    def _():
        m_ref[...] = jnp.full_like(m_ref, -jnp.inf)
        acc_ref[...] = jnp.zeros_like(acc_ref)

    def step(masked):
        qk = lambda j: _dot_nt(q_ref[0, j], k_ref[0, j])
        ss = {j: qk(j) for j in range(min(2, nh))}
        for j in range(nh):
            if j + 2 < nh:
                ss[j + 2] = qk(j + 2)
            s = ss.pop(j)
            if masked:
                s = jnp.where(_iota((t, t), 1) <= _iota((t, t), 0), s, -jnp.inf)
            m_prev = m_ref[j]
            m_new = jnp.maximum(m_prev, jnp.max(s, axis=1, keepdims=True))
            alpha = jnp.exp2(m_prev - m_new)
            pm = jnp.exp2(s - jnp.concatenate([m_new] * (t // LANES), axis=1))
            acc_ref[j] = alpha * acc_ref[j] + _dot(pm.astype(BF16), v_ref[0, j])
            m_ref[j] = m_new

    @pl.when(ki < qi)
    def _():
        step(False)

    @pl.when(ki == qi)
    def _():
        step(True)
        lane = _iota((t, LANES), 1)
        for jp in range(nh // 2):
            a0, a1 = acc_ref[2 * jp], acc_ref[2 * jp + 1]
            o0, o1 = a0 / a0[:, 64:65], a1 / a1[:, 64:65]
            o_ref[0, :, jp * LANES:(jp + 1) * LANES] = jnp.where(
                lane < 64, o0, pltpu.roll(o1, 64, 1)).astype(o_ref.dtype)


def _fox_attention(qa, ka, va):
    B, H, S, _ = qa.shape
    t = ATT_T
    nh = ATT_HEADS
    nq = S // t
    qi_arr = np.concatenate([np.full(i + 1, i, np.int32) for i in range(nq)])
    ki_arr = np.concatenate([np.arange(i + 1, dtype=np.int32) for i in range(nq)])
    n_pairs = qi_arr.shape[0]
    grid_spec = pltpu.PrefetchScalarGridSpec(
        num_scalar_prefetch=2,
        grid=(B, H // nh, n_pairs),
        in_specs=[pl.BlockSpec((1, nh, t, LANES), lambda b, hp, p, qi, ki: (b, hp, qi[p], 0)),
                  pl.BlockSpec((1, nh, t, LANES), lambda b, hp, p, qi, ki: (b, hp, ki[p], 0)),
                  pl.BlockSpec((1, nh, t, LANES), lambda b, hp, p, qi, ki: (b, hp, ki[p], 0))],
        out_specs=pl.BlockSpec((1, t, nh * HEAD_DIM), lambda b, hp, p, qi, ki: (b, qi[p], hp)),
        scratch_shapes=[pltpu.VMEM((nh, t, LANES), F32), pltpu.VMEM((nh, t, LANES), F32)],
    )
    return pl.pallas_call(
        _fox_kernel,
        grid_spec=grid_spec,
        out_shape=jax.ShapeDtypeStruct((B, S, FOX_W), BF16),
        compiler_params=_cparams(("arbitrary", "arbitrary", "arbitrary")),
        name="fox_attn",
    )(jnp.asarray(qi_arr), jnp.asarray(ki_arr), qa, ka, va)


_RV_W0, _RV_A0, _RV_KK, _RV_KA, _RV_RK, _RV_LNG, _RV_LNB = range(7)


def _head_stack(x, n_heads):
    lane = _iota(x.shape, 1)
    return jnp.concatenate(
        [jnp.where(lane // HEAD_DIM == h, x, 0.0) for h in range(n_heads)], axis=0)


def _rwkv_kernel(p_ref, mu_ref, vec_ref, w2_ref, a2_ref, g2_ref, lt_ref, bo_ref, o_ref,
                 prev_ref, s_ref, at_ref, rt_ref, bt_ref, kt_ref, v_ref, wl_ref, y_ref, bon_ref, g_ref,
                 *chunk_bufs):
    nb, ts, _ = p_ref.shape
    C = RWKV_C
    H = RWKV_HEADS
    W = RWKV_W

    @pl.when(pl.program_id(0) == 0)
    def _():
        prev_ref[...] = jnp.zeros_like(prev_ref)
        s_ref[...] = jnp.zeros_like(s_ref)

    vec = vec_ref[...]
    row = lambda i: vec[i:i + 1, :]
    bo = bo_ref[...]

    for b in range(nb):
        p = p_ref[b]
        rowi = _iota(p.shape, 0)
        prev = jnp.where(rowi == 0, jnp.broadcast_to(prev_ref[b, 0:1, :], p.shape), pltpu.roll(p, 1, 0))
        prev_ref[b] = jnp.broadcast_to(p[ts - 1:ts, :], prev_ref.shape[1:])
        xm = p + (prev - p) * mu_ref[...]
        r, k, v = xm[:, 0:W], xm[:, W:2 * W], xm[:, 2 * W:3 * W]
        wd = xm[:, 3 * W:3 * W + W_LORA]
        ad = xm[:, 3 * W + W_LORA:3 * W + W_LORA + A_LORA]
        gd = xm[:, 3 * W + W_LORA + A_LORA:]
        lw = row(_RV_W0) + _dot3(jnp.tanh(wd), w2_ref[...])
        logdec = -jnp.exp(_log_sigmoid(lw) - 0.5)
        a = _sigmoid(row(_RV_A0) + _dot3(ad, a2_ref[...]))
        g_ref[b] = _dot3(_sigmoid(gd), g2_ref[...])
        kk = k * row(_RV_KK)
        kk = kk / jnp.maximum(jnp.sqrt(_dot_01(kk * kk, bo, 2)), 1e-12)
        k2 = k * (1.0 + (a - 1.0) * row(_RV_KA))
        cum = _dot_01(logdec, lt_ref[...], 3, left=True)
        wincl = jnp.exp(cum)
        winv = jnp.exp(-cum)
        at_ref[b] = -kk * jnp.exp(cum - logdec)
        rt_ref[b] = r * wincl
        bt_ref[b] = kk * a * winv
        kt_ref[b] = k2 * winv
        v_ref[b] = v
        wl_ref[b] = wincl
        bon_ref[b] = r * k2 * row(_RV_RK)

    li = _iota((C, H * C), 1) % C
    ri = _iota((C, H * C), 0)
    strict = li < ri
    incl = li <= ri
    bd = (_iota((W, W), 0) // HEAD_DIM) == (_iota((W, W), 1) // HEAD_DIM)
    n_chunks = ts // C

    def prep(ci, bufs):
        pab_ref, prb_ref, base_ref, y0_ref = bufs
        sl = pl.ds(pl.multiple_of(ci * C, C), C)
        for b in range(nb):
            at, rt, bt, kt, vv = at_ref[b, sl, :], rt_ref[b, sl, :], bt_ref[b, sl, :], kt_ref[b, sl, :], v_ref[b, sl, :]
            bstack, kstack, vstack = _head_stack(bt, H), _head_stack(kt, H), _head_stack(vv, H)
            pab_ref[b] = jnp.where(strict, _dot3(at, bstack, nt=True), 0.0)
            pak = jnp.where(strict, _dot3(at, kstack, nt=True), 0.0)
            prb_ref[b] = jnp.where(incl, _dot3(rt, bstack, nt=True), 0.0)
            prk = jnp.where(incl, _dot3(rt, kstack, nt=True), 0.0)
            base_ref[b] = _dot3(pak, vstack)
            y0_ref[b] = _dot3(prk, vstack)

    def seq(ci, bufs):
        pab_ref, prb_ref, base_ref, y0_ref = bufs
        c0 = pl.multiple_of(ci * C, C)
        sl = pl.ds(c0, C)
        s0 = [s_ref[b] for b in range(nb)]
        ar = [_dot3(jnp.concatenate([at_ref[b, sl, :], rt_ref[b, sl, :]], axis=0), s0[b], nt=True)
              for b in range(nb)]
        half = C // 2
        lo_lanes = _iota((8, LANES), 1) < HEAD_DIM
        lo_half = _iota((half, LANES), 1) < HEAD_DIM
        u = {}
        for b in range(nb):
            base = base_ref[b] + ar[b][0:C]
            for hp in range(H // 2):
                u[b, hp] = [base[8 * j:8 * j + 8, hp * LANES:(hp + 1) * LANES] for j in range(C // 8)]
        pab = {(b, hp): pab_ref[b, :, hp * LANES:(hp + 1) * LANES] for b in range(nb) for hp in range(H // 2)}

        def solve_half(t0):
            for t in range(t0, t0 + half - 1):
                for key, w in u.items():
                    u_t = jnp.broadcast_to(w[t // 8][t % 8:t % 8 + 1, :], (8, LANES))
                    for j in range((t + 1) // 8, (t0 + half) // 8):
                        rows = pab[key][8 * j:8 * j + 8, :]
                        coef = jnp.where(lo_lanes, jnp.broadcast_to(rows[:, t:t + 1], (8, LANES)),
                                         jnp.broadcast_to(rows[:, HEAD_DIM + t:HEAD_DIM + t + 1], (8, LANES)))
                        w[j] = w[j] + coef * u_t

        solve_half(0)
        for key, w in u.items():
            first = jnp.concatenate(w[0:half // 8], axis=0)
            a_lo = pab[key][half:C, 0:half]
            a_hi = pab[key][half:C, HEAD_DIM:HEAD_DIM + half]
            inc = jnp.where(lo_half, _dot3(a_lo, first), _dot3(a_hi, first))
            for j in range(half // 8, C // 8):
                w[j] = w[j] + inc[8 * j - half:8 * j - half + 8, :]
        solve_half(half)
        for b in range(nb):
            u_full = jnp.concatenate([jnp.concatenate(u[b, hp], axis=0) for hp in range(H // 2)], axis=1)
            y_ref[b, sl, :] = y0_ref[b] + ar[b][C:2 * C] + _dot3(prb_ref[b], _head_stack(u_full, H))
            upd = _dot3(jnp.concatenate([u_full, v_ref[b, sl, :]], axis=0).T,
                        jnp.concatenate([bt_ref[b, sl, :], kt_ref[b, sl, :]], axis=0))
            wl = wl_ref[b, pl.ds(c0 + C - 1, 1), :]
            s_ref[b] = (s0[b] + jnp.where(bd, upd, 0.0)) * wl

    bufs_a, bufs_b = chunk_bufs[0:4], chunk_bufs[4:8]
    prep(0, bufs_a)

    def pair_body(k, carry):
        ci = 2 * k
        prep(ci + 1, bufs_b)
        seq(ci, bufs_a)
        prep(jnp.minimum(ci + 2, n_chunks - 1), bufs_a)
        seq(ci + 1, bufs_b)
        return carry

    lax.fori_loop(0, n_chunks // 2, pair_body, 0)

    for b in range(nb):
        y = y_ref[b]
        mean = _dot_01(y, bo, 2) * (1.0 / HEAD_DIM)
        d = y - mean
        var = _dot_01(d * d, bo, 2) * (1.0 / HEAD_DIM)
        yn = d * lax.rsqrt(var + RWKV_GN_EPS) * row(_RV_LNG) + row(_RV_LNB)
        bonus = _dot_01(bon_ref[b], bo, 2) * v_ref[b]
        o_ref[b] = (yn + bonus) * g_ref[b]


def _rwkv(rw, prm, l):
    B, S, _ = rw.shape
    ts = RWKV_TS
    full = lambda a: pl.BlockSpec(a.shape, lambda s: (0,) * a.ndim)
    args = [prm["rwkv_mu"][l], prm["rwkv_vec"][l], prm["rwkv_w2"][l], prm["rwkv_a2"][l], prm["rwkv_g2"][l],
            prm["rwkv_lt"], prm["block_ones"]]
    big = lambda: pltpu.VMEM((B, ts, RWKV_W), F32)
    return pl.pallas_call(
        _rwkv_kernel,
        grid=(S // ts,),
        in_specs=[pl.BlockSpec((B, ts, 1024), lambda s: (0, s, 0))] + [full(a) for a in args],
        out_specs=pl.BlockSpec((B, ts, RWKV_W), lambda s: (0, s, 0)),
        out_shape=jax.ShapeDtypeStruct((B, S, RWKV_W), F32),
        scratch_shapes=[pltpu.VMEM((B, 8, 1024), F32), pltpu.VMEM((B, RWKV_W, RWKV_W), F32)]
                       + [big() for _ in range(9)]
                       + [pltpu.VMEM((B, RWKV_C, RWKV_W), F32) for _ in range(8)],
        compiler_params=_cparams(("arbitrary",)),
        name="rwkv",
    )(rw, *args)


def _ret_kernel(p_ref, cos_ref, sin_ref, dm_ref, qd_ref, kd_ref, dmat_ref, bo_ref, gn_ref, o_ref, r_ref):
    C = p_ref.shape[1]
    W = RET_W
    H = RET_HEADS

    @pl.when(pl.program_id(1) == 0)
    def _():
        r_ref[...] = jnp.zeros_like(r_ref)

    p = p_ref[0]
    cos, sin = cos_ref[...], sin_ref[...]
    first_half = (_iota((C, W), 1) % HEAD_DIM) < (HEAD_DIM // 2)

    def rope(x):
        partner = jnp.where(first_half, pltpu.roll(x, W - HEAD_DIM // 2, 1), pltpu.roll(x, HEAD_DIM // 2, 1))
        return x * cos + partner * sin

    q = rope(p[:, 0:W])
    k = rope(p[:, W:2 * W]) * (HEAD_DIM ** -0.5)
    v = p[:, 2 * W:3 * W]
    g = p[:, 3 * W:4 * W]
    state = r_ref[...]
    scores = _dot_nt(q, _head_stack(k, H)) * dm_ref[...]
    y = _dot(scores, _head_stack(v, H)) + _dot(q * qd_ref[...], state)
    kv = _dot((k * kd_ref[...]).T, v)
    bd = (_iota((W, W), 0) // HEAD_DIM) == (_iota((W, W), 1) // HEAD_DIM)
    r_ref[...] = state * dmat_ref[...] + jnp.where(bd, kv, 0.0)

    bo = bo_ref[...]
    mean = _dot_01(y, bo, 2) * (1.0 / HEAD_DIM)
    d = y - mean
    var = _dot_01(d * d, bo, 2) * (1.0 / HEAD_DIM)
    o_ref[0] =d * lax.rsqrt(var + RET_GN_EPS) * gn_ref[...] * (g * _sigmoid(g))


def _retention(rt, prm, l):
    B, S, _ = rt.shape
    C = RET_C
    cst = lambda a: pl.BlockSpec(a.shape, lambda b, s: (0,) * a.ndim)
    consts = [prm["ret_dmask"], prm["ret_qdec"], prm["ret_kdec"], prm["ret_dmat"], prm["block_ones"],
              prm["ret_gn"][l]]
    return pl.pallas_call(
        _ret_kernel,
        grid=(B, S // C),
        in_specs=[pl.BlockSpec((1, C, 1024), lambda b, s: (b, s, 0)),
                  pl.BlockSpec((C, RET_W), lambda b, s: (s, 0)),
                  pl.BlockSpec((C, RET_W), lambda b, s: (s, 0))] + [cst(a) for a in consts],
        out_specs=pl.BlockSpec((1, C, RET_W), lambda b, s: (b, s, 0)),
        out_shape=jax.ShapeDtypeStruct((B, S, RET_W), F32),
        scratch_shapes=[pltpu.VMEM((RET_W, RET_W), F32)],
        compiler_params=_cparams(("arbitrary", "arbitrary")),
        name="retention",
    )(rt, prm["ret_cos"], prm["ret_sin"], *consts)


def _ret_tables(S):
    C, H, d = RET_C, RET_HEADS, HEAD_DIM
    half = d // 2
    inv = ROPE_BASE ** (-jnp.arange(half, dtype=F32) / half)
    ang = jnp.arange(S, dtype=F32)[:, None] * inv[None, :]
    cos, sin = jnp.cos(ang), jnp.sin(ang)
    cos_t = jnp.tile(jnp.concatenate([cos, cos], axis=1), (1, H))
    sin_t = jnp.tile(jnp.concatenate([-sin, sin], axis=1), (1, H))
    log_gamma = jnp.log1p(-jnp.exp2(-5.0 - jnp.arange(H, dtype=F32)))
    idx = jnp.arange(C, dtype=F32)
    diff = idx[:, None] - idx[None, :]
    dmask = jnp.where(diff >= 0, jnp.exp(jnp.maximum(diff, 0.0) * log_gamma[:, None, None]), 0.0)
    dmask = jnp.transpose(dmask, (1, 0, 2)).reshape(C, H * C)
    rep = lambda a: jnp.repeat(a, d, axis=1)
    kdec = rep(jnp.exp((C - 1.0 - idx)[:, None] * log_gamma[None, :]))
    qdec = rep(jnp.exp((idx + 1.0)[:, None] * log_gamma[None, :]))
    hid = np.arange(H * d) // d
    bd = jnp.asarray(hid[:, None] == hid[None, :])
    dmat = jnp.where(bd, jnp.repeat(jnp.exp(C * log_gamma), d)[:, None], 0.0)
    return {"ret_cos": cos_t, "ret_sin": sin_t, "ret_dmask": dmask, "ret_qdec": qdec, "ret_kdec": kdec,
            "ret_dmat": dmat}


def _layer_norm(z, g, b):
    mu = jnp.mean(z, axis=1, keepdims=True)
    d = z - mu
    var = jnp.mean(d * d, axis=1, keepdims=True)
    return d * lax.rsqrt(var + LN_EPS) * g + b


ROW_TILES = D_MODEL // LANES


def _out_kernel(yf_ref, yr_ref, yt_ref, x_ref, w_ref, g_ref, b_ref, wr_ref, br_ref,
                x1_ref, info_ref, info_t_ref):
    tm = x_ref.shape[0]
    mixed = (_dot(yf_ref[...], w_ref[0:FOX_W, :])
             + _dot(yr_ref[...].astype(BF16), w_ref[FOX_W:FOX_W + RWKV_W, :])
             + _dot(yt_ref[...].astype(BF16), w_ref[FOX_W + RWKV_W:, :]))
    x1 = _layer_norm(ALPHA * x_ref[...] + mixed, g_ref[...], b_ref[...])
    x1_ref[...] = x1

    logits = _dot3(x1, wr_ref[...]) + br_ref[...]
    lane = _iota((tm, LANES), 1).astype(F32)
    ninf = -jnp.inf
    first = lambda hit: jnp.min(jnp.where(hit, lane, float(LANES)), axis=1, keepdims=True)
    gl = jnp.where(lane < N_GROUPS, logits, ninf)
    gmax = jnp.max(gl, axis=1, keepdims=True)
    grp = first(gl == gmax)
    g_gate = 1.0 / jnp.sum(jnp.exp(gl - gmax), axis=1, keepdims=True)
    lo = N_GROUPS + grp * EXPERTS_PER_GROUP
    el = jnp.where(jnp.logical_and(lane >= lo, lane < lo + EXPERTS_PER_GROUP), logits, ninf)
    v1 = jnp.max(el, axis=1, keepdims=True)
    i1 = first(el == v1)
    el2 = jnp.where(lane == i1, ninf, el)
    v2 = jnp.max(el2, axis=1, keepdims=True)
    i2 = first(el2 == v2)
    e = jnp.exp(v2 - v1)
    den = 1.0 / (1.0 + e)
    info = jnp.where(lane == 0, i1 - N_GROUPS, jnp.where(lane == 1, i2 - N_GROUPS,
           jnp.where(lane == 2, g_gate * den, jnp.where(lane == 3, g_gate * e * den, 0.0))))
    info_ref[...] = info
    info_t_ref[...] = info.T[0:8, :]


def _out_proj(yf, yr, yt, x, prm, l):
    T = x.shape[0]
    tm = OUT_TM
    rows = lambda w: pl.BlockSpec((tm, w), lambda i: (i, 0))
    cst = lambda a: pl.BlockSpec(a.shape, lambda i: (0,) * a.ndim)
    consts = [prm["w_out"][l], prm["ln1_g"][l], prm["ln1_b"][l], prm["w_router"][l], prm["b_router"][l]]
    return pl.pallas_call(
        _out_kernel,
        grid=(T // tm,),
        in_specs=[rows(FOX_W), rows(RWKV_W), rows(RET_W), rows(D_MODEL)] + [cst(a) for a in consts],
        out_specs=[rows(D_MODEL), rows(LANES), pl.BlockSpec((8, tm), lambda i: (0, i))],
        out_shape=[jax.ShapeDtypeStruct((T, D_MODEL), F32), jax.ShapeDtypeStruct((T, LANES), F32),
                   jax.ShapeDtypeStruct((8, T), F32)],
        compiler_params=_cparams(("arbitrary",)),
        name="out_proj",
    )(yf, yr, yt, x, *consts)


def _rank_kernel(it_ref, tri_ref, rk_ref, cnt_ref, carry_ref):
    tm = it_ref.shape[1]

    @pl.when(pl.program_id(0) == 0)
    def _():
        carry_ref[...] = jnp.zeros_like(carry_ref)

    sub = _iota((N_EXPERTS, tm), 0).astype(F32)
    oh1 = sub == it_ref[0:1, :]
    oh2 = sub == it_ref[1:2, :]
    oh = jnp.where(jnp.logical_or(oh1, oh2), 1.0, 0.0)
    before = _dot(oh.astype(BF16), tri_ref[...]) + carry_ref[:, 0:1]
    r1 = jnp.sum(jnp.where(oh1, before, 0.0), axis=0, keepdims=True)
    r2 = jnp.sum(jnp.where(oh2, before, 0.0), axis=0, keepdims=True)
    rk_ref[...] = jnp.concatenate([r1, r2, jnp.zeros((6, tm), F32)], axis=0)
    carry_ref[...] = carry_ref[...] + jnp.sum(oh, axis=1, keepdims=True)
    cnt_ref[...] = carry_ref[...]


def _rank(info_t, tri):
    T = info_t.shape[1]
    tm = RANK_TM
    return pl.pallas_call(
        _rank_kernel,
        grid=(T // tm,),
        in_specs=[pl.BlockSpec((8, tm), lambda i: (0, i)), pl.BlockSpec((tm, tm), lambda i: (0, 0))],
        out_specs=[pl.BlockSpec((8, tm), lambda i: (0, i)), pl.BlockSpec((N_EXPERTS, LANES), lambda i: (0, 0))],
        out_shape=[jax.ShapeDtypeStruct((8, T), F32), jax.ShapeDtypeStruct((N_EXPERTS, LANES), F32)],
        scratch_shapes=[pltpu.VMEM((N_EXPERTS, LANES), F32)],
        compiler_params=_cparams(("arbitrary",)),
        name="rank",
    )(info_t, tri)


def _row_copy(src_hbm, row, dst, r, sem):
    return pltpu.make_async_copy(src_hbm.at[row], dst.at[:, r, :], sem)


def _expert_kernel(be_ref, nu_ref, src0_ref, src1_ref, x_hbm, w1_ref, w3_ref, w2_ref, y_ref, xbuf, sem):
    j = pl.program_id(0)
    tm = xbuf.shape[2]
    nu = nu_ref[0]
    slot = j % 2

    def gather(src_ref, s):
        for r in range(tm):
            _row_copy(x_hbm, src_ref[0, 0, r], xbuf.at[s], r, sem.at[s]).start()

    def drain(s):
        def body(r, c):
            _row_copy(x_hbm, 0, xbuf.at[s], r, sem.at[s]).wait()
            return c
        lax.fori_loop(0, tm, body, 0, unroll=8)

    @pl.when(j == 0)
    def _():
        gather(src0_ref, 0)

    @pl.when(j < nu)
    def _():
        drain(slot)
        x = jnp.concatenate([xbuf[slot, c] for c in range(ROW_TILES)], axis=1).astype(BF16)
        gather(src1_ref, 1 - slot)
        a = _dot(x, w1_ref[0])
        h = (a * _sigmoid(a)) * _dot(x, w3_ref[0])
        y = _dot(h.astype(BF16), w2_ref[0])
        for c in range(ROW_TILES):
            y_ref[:, c, :] = y[:, c * LANES:(c + 1) * LANES]

    @pl.when(j == nu)
    def _():
        drain(slot)

    @pl.when(j >= nu)
    def _():
        y_ref[...] = jnp.zeros_like(y_ref)


def _experts(block_expert, n_used, src_tok, x1, w1, w3, w2):
    n_blk = block_expert.shape[0]
    tm = MOE_TM
    src_tok = src_tok.reshape(n_blk, 1, tm)
    grid_spec = pltpu.PrefetchScalarGridSpec(
        num_scalar_prefetch=2,
        grid=(n_blk,),
        in_specs=[pl.BlockSpec((1, 1, tm), lambda j, be, nu: (0, 0, 0), memory_space=pltpu.SMEM),
                  pl.BlockSpec((1, 1, tm), lambda j, be, nu: (jnp.minimum(j + 1, n_blk - 1), 0, 0),
                               memory_space=pltpu.SMEM),
                  pl.BlockSpec(memory_space=pl.ANY),
                  pl.BlockSpec((1, D_MODEL, D_EXPERT), lambda j, be, nu: (be[j], 0, 0)),
                  pl.BlockSpec((1, D_MODEL, D_EXPERT), lambda j, be, nu: (be[j], 0, 0)),
                  pl.BlockSpec((1, D_EXPERT, D_MODEL), lambda j, be, nu: (be[j], 0, 0))],
        out_specs=pl.BlockSpec((tm, ROW_TILES, LANES), lambda j, be, nu: (j, 0, 0)),
        scratch_shapes=[pltpu.VMEM((2, ROW_TILES, tm, LANES), F32), pltpu.SemaphoreType.DMA((2,))],
    )
    return pl.pallas_call(
        _expert_kernel,
        grid_spec=grid_spec,
        out_shape=jax.ShapeDtypeStruct((n_blk * tm, ROW_TILES, LANES), F32),
        compiler_params=_cparams(("arbitrary",)),
        name="experts",
    )(block_expert, n_used, src_tok, src_tok, x1, w1, w3, w2)


def _combine_kernel(dest0_ref, dest1_ref, y_hbm, info_ref, x_ref, g_ref, b_ref, o_ref, ybuf, sem):
    i = pl.program_id(0)
    tm = x_ref.shape[0]
    slot = i % 2

    def gather(dest_ref, s):
        for r in range(2 * tm):
            _row_copy(y_hbm, dest_ref[0, 0, r], ybuf.at[s], r, sem.at[s]).start()

    @pl.when(i == 0)
    def _():
        gather(dest0_ref, 0)

    @pl.when(i + 1 < pl.num_programs(0))
    def _():
        gather(dest1_ref, 1 - slot)

    def drain(r, c):
        _row_copy(y_hbm, 0, ybuf.at[slot], r, sem.at[slot]).wait()
        return c
    lax.fori_loop(0, 2 * tm, drain, 0, unroll=8)
    info = info_ref[...]
    rows = lambda lo: jnp.concatenate([ybuf[slot, c, pl.ds(lo, tm), :] for c in range(ROW_TILES)], axis=1)
    moe = info[:, 2:3] * rows(0) + info[:, 3:4] * rows(tm)
    o_ref[...] = _layer_norm(ALPHA * x_ref[...] + moe, g_ref[...], b_ref[...])


def _combine(dest, y_pad, info, x1, g, b):
    T = x1.shape[0]
    tm = COMB_TM
    cst = lambda a: pl.BlockSpec(a.shape, lambda i: (0,) * a.ndim)
    n = T // tm
    return pl.pallas_call(
        _combine_kernel,
        grid=(n,),
        in_specs=[pl.BlockSpec((1, 1, 2 * tm), lambda i: (0, 0, 0), memory_space=pltpu.SMEM),
                  pl.BlockSpec((1, 1, 2 * tm), lambda i: (jnp.minimum(i + 1, n - 1), 0, 0),
                               memory_space=pltpu.SMEM),
                  pl.BlockSpec(memory_space=pl.ANY),
                  pl.BlockSpec((tm, LANES), lambda i: (i, 0)),
                  pl.BlockSpec((tm, D_MODEL), lambda i: (i, 0)), cst(g), cst(b)],
        out_specs=pl.BlockSpec((tm, D_MODEL), lambda i: (i, 0)),
        out_shape=jax.ShapeDtypeStruct((T, D_MODEL), F32),
        scratch_shapes=[pltpu.VMEM((2, ROW_TILES, 2 * tm, LANES), F32), pltpu.SemaphoreType.DMA((2,))],
        compiler_params=_cparams(("arbitrary",)),
        name="combine",
    )(dest, dest, y_pad, info, x1, g, b)


def _moe(x1, x1t, info, info_t, prm, l):
    T = x1.shape[0]
    tm = MOE_TM
    n_blk = (2 * T) // tm + N_EXPERTS
    ranks, cnt = _rank(info_t, prm["tri_rank"])
    counts = cnt[:, 0].astype(jnp.int32)
    padded = ((counts + tm - 1) // tm) * tm
    pad_end = jnp.cumsum(padded)
    pad_off = pad_end - padded
    eid = info_t[0:2].astype(jnp.int32)
    e_iota = jnp.arange(N_EXPERTS, dtype=jnp.int32)
    off = jnp.sum(jnp.where(eid[:, None, :] == e_iota[None, :, None], pad_off[None, :, None], 0), axis=1)
    dest = off + ranks[0:2].astype(jnp.int32)
    tok = jnp.broadcast_to(jnp.arange(T, dtype=jnp.int32), (2, T))
    src_tok = jnp.zeros((n_blk * tm,), jnp.int32).at[dest.reshape(-1)].set(tok.reshape(-1))
    blk_start = jnp.arange(n_blk, dtype=jnp.int32) * tm
    block_expert = jnp.minimum(jnp.sum((pad_end[None, :] <= blk_start[:, None]).astype(jnp.int32), axis=1),
                               N_EXPERTS - 1)
    n_used = (pad_end[-1] // tm).astype(jnp.int32).reshape(1)
    block_expert = jnp.where(jnp.arange(n_blk) < n_used[0], block_expert,
                             block_expert[jnp.maximum(n_used[0] - 1, 0)])
    y_pad = _experts(block_expert, n_used, src_tok, x1t, prm["moe_w1"][l], prm["moe_w3"][l], prm["moe_w2"][l])
    ct = COMB_TM
    dest_blk = dest.reshape(2, T // ct, ct).transpose(1, 0, 2).reshape(T // ct, 1, 2 * ct)
    return _combine(dest_blk, y_pad, info, x1, prm["ln2_g"][l], prm["ln2_b"][l])


def _prep_params(p):
    L = p["w_in"].shape[0]
    w = p["w_in"]
    fg = jnp.pad(w[:, :, 1536:1544], ((0, 0), (0, 0), (0, LANES - FOX_HEADS)))
    w_in = jnp.concatenate([w[:, :, 0:1536], fg, w[:, :, 1544:2568], w[:, :, 2568:3592]], axis=-1)
    out = {
        "w_in": w_in.astype(BF16),
        "fbias": jnp.pad(p["fox_fgate_bias"], ((0, 0), (0, LANES - FOX_HEADS)))[:, None, :],
        "tri_proj": jnp.tril(jnp.ones((PROJ_TM, PROJ_TM), BF16)),
    }
    vec = jnp.stack([p["rwkv_w0"], p["rwkv_a0"], p["rwkv_k_k"], p["rwkv_k_a"],
                     p["rwkv_r_k"].reshape(L, RWKV_W), p["rwkv_ln_g"], p["rwkv_ln_b"],
                     jnp.zeros((L, RWKV_W), F32)], axis=1)
    ti = np.arange(RWKV_TS)
    lt = ((ti[:, None] // RWKV_C == ti[None, :] // RWKV_C) & (ti[None, :] <= ti[:, None])).astype(np.float32)
    hi = np.arange(RWKV_W) // HEAD_DIM
    out.update({
        "rwkv_mu": jnp.concatenate([p["rwkv_mu_rkv"].reshape(L, 3 * RWKV_W), p["rwkv_mu_lora"]], axis=-1)[:, None, :],
        "rwkv_vec": vec,
        "rwkv_w2": p["rwkv_w2"], "rwkv_a2": p["rwkv_a2"], "rwkv_g2": p["rwkv_g2"],
        "rwkv_lt": jnp.asarray(lt, dtype=BF16),
        "block_ones": jnp.asarray((hi[:, None] == hi[None, :]).astype(np.float32), dtype=BF16),
    })
    out.update(_ret_tables(p["x"].shape[1]))
    out["ret_gn"] = p["ret_gn_g"][:, None, :]
    pad_r = LANES - N_GROUPS - N_EXPERTS
    ri = np.arange(RANK_TM)
    out.update({
        "w_out": p["w_out"].astype(BF16),
        "ln1_g": p["ln1_g"][:, None, :], "ln1_b": p["ln1_b"][:, None, :],
        "ln2_g": p["ln2_g"][:, None, :], "ln2_b": p["ln2_b"][:, None, :],
        "w_router": jnp.pad(jnp.concatenate([p["moe_w_group"], p["moe_w_expert"]], axis=-1),
                            ((0, 0), (0, 0), (0, pad_r))),
        "b_router": jnp.pad(jnp.concatenate([p["moe_b_group"], p["moe_b_expert"]], axis=-1),
                            ((0, 0), (0, pad_r)))[:, None, :],
        "tri_rank": jnp.asarray((ri[:, None] < ri[None, :]).astype(np.float32), dtype=BF16),
        "moe_w1": p["moe_w1"].astype(BF16), "moe_w3": p["moe_w3"].astype(BF16),
        "moe_w2": p["moe_w2"].astype(BF16),
    })
    return out


def kernel(x, w_in, fox_fgate_bias, rwkv_mu_rkv, rwkv_mu_lora, rwkv_w0, rwkv_w2, rwkv_a0, rwkv_a2, rwkv_g2,
           rwkv_k_k, rwkv_k_a, rwkv_r_k, rwkv_ln_g, rwkv_ln_b, ret_gn_g, w_out, ln1_g, ln1_b, ln2_g, ln2_b,
           moe_w_group, moe_b_group, moe_w_expert, moe_b_expert, moe_w1, moe_w3, moe_w2):
    prm = _prep_params(dict(
        x=x, w_in=w_in, fox_fgate_bias=fox_fgate_bias, rwkv_mu_rkv=rwkv_mu_rkv, rwkv_mu_lora=rwkv_mu_lora,
        rwkv_w0=rwkv_w0, rwkv_w2=rwkv_w2, rwkv_a0=rwkv_a0, rwkv_a2=rwkv_a2, rwkv_g2=rwkv_g2,
        rwkv_k_k=rwkv_k_k, rwkv_k_a=rwkv_k_a, rwkv_r_k=rwkv_r_k, rwkv_ln_g=rwkv_ln_g, rwkv_ln_b=rwkv_ln_b,
        ret_gn_g=ret_gn_g, w_out=w_out, ln1_g=ln1_g, ln1_b=ln1_b, ln2_g=ln2_g, ln2_b=ln2_b,
        moe_w_group=moe_w_group, moe_b_group=moe_b_group, moe_w_expert=moe_w_expert,
        moe_b_expert=moe_b_expert, moe_w1=moe_w1, moe_w3=moe_w3, moe_w2=moe_w2))
    B, S, D = x.shape
    T = B * S
    for l in range(w_in.shape[0]):
        qa, ka, va, rw, rt = _project(x, prm["w_in"][l], prm["fbias"][l], prm["tri_proj"])
        y_fox = _fox_attention(qa, ka, va)
        y_rwkv = _rwkv(rw, prm, l)
        y_ret = _retention(rt, prm, l)
        x1, info, info_t = _out_proj(y_fox.reshape(T, FOX_W), y_rwkv.reshape(T, RWKV_W),
                                     y_ret.reshape(T, RET_W), x.reshape(T, D), prm, l)
        x = _moe(x1, x1.reshape(T, ROW_TILES, LANES), info, info_t, prm, l).reshape(B, S, D)
    return x
```

```python
import functools
import math

import jax
import jax.numpy as jnp
import numpy as np
from jax import lax
from jax.experimental import pallas as pl
from jax.experimental.pallas import tpu as pltpu

F32 = jnp.float32
BF16 = jnp.bfloat16
HIGHEST = lax.Precision.HIGHEST

D_MODEL = 1024
DEPTH = 4
HEAD_DIM = 64
FOX_W, RWKV_W, RET_W = 512, 256, 256
FOX_HEADS, RWKV_HEADS, RET_HEADS = 8, 4, 4
W_LORA, A_LORA, G_LORA = 64, 64, 128
ROPE_BASE = 10000.0
N_GROUPS, EXPERTS_PER_GROUP, N_EXPERTS, D_EXPERT = 4, 8, 32, 512
ALPHA = (2.0 * DEPTH) ** 0.25
LN_EPS = 1e-5
LOG2E = math.log2(math.e)
RWKV_GN_EPS = 64e-5
RET_GN_EPS = 1e-6

LANES = 128
VMEM_LIMIT = 56 * 1024 * 1024

PROJ_TM = 512
ATT_T = 1024
ATT_HEADS = 8
RWKV_TS = 512
RWKV_C = 64
RET_C = 256
OUT_TM = 512
RANK_TM = 512
MOE_TM = 256
COMB_TM = 256

P_FOX = 0
P_FG = 1536
P_RWKV = 1664
P_RET = 2688
P_TOT = 3712


def _cparams(sem):
    return pltpu.CompilerParams(dimension_semantics=sem, vmem_limit_bytes=VMEM_LIMIT)


def _dot(a, b, precision=None):
    return jnp.dot(a, b, preferred_element_type=F32, precision=precision)


def _dot_nt(a, b, precision=None):
    return lax.dot_general(a, b, (((1,), (1,)), ((), ())), preferred_element_type=F32,
                           precision=precision)


def _split(a, n):
    parts, r = [], a
    for i in range(n):
        h = r.astype(BF16)
        parts.append(h)
        if i + 1 < n:
            r = r - h.astype(F32)
    return parts


def _dot3(a, b, nt=False):
    d = _dot_nt if nt else _dot
    a1, a2 = _split(a, 2)
    b1, b2 = _split(b, 2)
    return (d(a1, b2) + d(a2, b1)) + d(a1, b1)


def _dot_01(a, m, n=3, left=False):
    d = (lambda p: _dot(m, p)) if left else (lambda p: _dot(p, m))
    parts = _split(a, n)
    acc = d(parts[-1])
    for p in parts[-2::-1]:
        acc = acc + d(p)
    return acc


def _log_sigmoid(z):
    return -(jnp.maximum(-z, 0.0) + jnp.log(1.0 + jnp.exp(-jnp.abs(z))))


def _sigmoid(z):
    return 1.0 / (1.0 + jnp.exp(-z))


def _iota(shape, dim):
    return lax.broadcasted_iota(jnp.int32, shape, dim)


def _proj_kernel(x_ref, w_ref, fb_ref, tri_ref, qa_ref, ka_ref, va_ref, rw_ref, rt_ref, carry_ref):
    tm = x_ref.shape[1]

    @pl.when(pl.program_id(1) == 0)
    def _():
        carry_ref[...] = jnp.zeros_like(carry_ref)

    xb = x_ref[0].astype(BF16)
    p_fox = _dot(xb, w_ref[:, P_FOX:P_FG])
    z = _dot(xb, w_ref[:, P_FG:P_RWKV]) + fb_ref[...]
    rw_ref[0] = _dot(xb, w_ref[:, P_RWKV:P_RET])
    rt_ref[0] = _dot(xb, w_ref[:, P_RET:P_TOT])

    c = _dot_01(_log_sigmoid(z), tri_ref[...], 3, left=True) + carry_ref[0:1, :]
    carry_ref[...] = jnp.broadcast_to(c[tm - 1:tm, :], carry_ref.shape)

    lane = _iota((tm, LANES), 1)
    for h in range(FOX_HEADS):
        cb = jnp.broadcast_to(c[:, h:h + 1], (tm, LANES)) * LOG2E
        hi = cb.astype(BF16).astype(F32)
        r1 = cb - hi
        mid = r1.astype(BF16).astype(F32)
        lo = r1 - mid
        tile = (h // 2) * LANES

        def head_lanes(base):
            t = p_fox[:, base + tile: base + tile + LANES]
            return pltpu.roll(t, 64, 1) if h % 2 else t

        q = head_lanes(0) * (LOG2E * HEAD_DIM ** -0.5)
        k = head_lanes(FOX_W)
        v = head_lanes(2 * FOX_W)
        qa = jnp.where(lane < 64, q, jnp.where(lane == 64, hi, jnp.where(lane == 65, mid,
             jnp.where(lane == 66, lo, jnp.where(lane < 70, 1.0, 0.0)))))
        ka = jnp.where(lane < 64, k, jnp.where(lane < 67, 1.0, jnp.where(lane == 67, -hi,
             jnp.where(lane == 68, -mid, jnp.where(lane == 69, -lo, 0.0)))))
        va = jnp.where(lane < 64, v, jnp.where(lane == 64, 1.0, 0.0))
        qa_ref[0, h] = qa.astype(BF16)
        ka_ref[0, h] = ka.astype(BF16)
        va_ref[0, h] = va.astype(BF16)


def _project(x, w_bf16, fbias, tri):
    B, S, D = x.shape
    tm = PROJ_TM
    aug = jax.ShapeDtypeStruct((B, FOX_HEADS, S, LANES), BF16)
    wide = jax.ShapeDtypeStruct((B, S, 1024), F32)
    aug_spec = pl.BlockSpec((1, FOX_HEADS, tm, LANES), lambda b, s: (b, 0, s, 0))
    wide_spec = pl.BlockSpec((1, tm, 1024), lambda b, s: (b, s, 0))
    return pl.pallas_call(
        _proj_kernel,
        grid=(B, S // tm),
        in_specs=[pl.BlockSpec((1, tm, D), lambda b, s: (b, s, 0)),
                  pl.BlockSpec((D, P_TOT), lambda b, s: (0, 0)),
                  pl.BlockSpec((1, LANES), lambda b, s: (0, 0)),
                  pl.BlockSpec((tm, tm), lambda b, s: (0, 0))],
        out_specs=[aug_spec, aug_spec, aug_spec, wide_spec, wide_spec],
        out_shape=[aug, aug, aug, wide, wide],
        scratch_shapes=[pltpu.VMEM((8, LANES), F32)],
        compiler_params=_cparams(("arbitrary", "arbitrary")),
        name="proj",
    )(x, w_bf16, fbias, tri)


def _fox_kernel(qi_ref, ki_ref, q_ref, k_ref, v_ref, o_ref, m_ref, acc_ref):
    p = pl.program_id(2)
    qi = qi_ref[p]
    ki = ki_ref[p]
    nh, t = q_ref.shape[1], q_ref.shape[2]

    @pl.when(ki == 0)
    def _():
        m_ref[...] = jnp.full_like(m_ref, -jnp.inf)
        acc_ref[...] = jnp.zeros_like(acc_ref)

    def step(masked):
        qk = lambda j: _dot_nt(q_ref[0, j], k_ref[0, j])
        ss = {j: qk(j) for j in range(min(2, nh))}
        for j in range(nh):
            if j + 2 < nh:
                ss[j + 2] = qk(j + 2)
            s = ss.pop(j)
            if masked:
                s = jnp.where(_iota((t, t), 1) <= _iota((t, t), 0), s, -jnp.inf)
            m_prev = m_ref[j]
            m_new = jnp.maximum(m_prev, jnp.max(s, axis=1, keepdims=True))
            alpha = jnp.exp2(m_prev - m_new)
            pm = jnp.exp2(s - jnp.concatenate([m_new] * (t // LANES), axis=1))
            acc_ref[j] = alpha * acc_ref[j] + _dot(pm.astype(BF16), v_ref[0, j])
            m_ref[j] = m_new

    @pl.when(ki < qi)
    def _():
        step(False)

    @pl.when(ki == qi)
    def _():
        step(True)
        lane = _iota((t, LANES), 1)
        for jp in range(nh // 2):
            a0, a1 = acc_ref[2 * jp], acc_ref[2 * jp + 1]
            o0, o1 = a0 / a0[:, 64:65], a1 / a1[:, 64:65]
            o_ref[0, :, jp * LANES:(jp + 1) * LANES] = jnp.where(
                lane < 64, o0, pltpu.roll(o1, 64, 1)).astype(o_ref.dtype)


def _fox_attention(qa, ka, va):
    B, H, S, _ = qa.shape
    t = ATT_T
    nh = ATT_HEADS
    nq = S // t
    qi_arr = np.concatenate([np.full(i + 1, i, np.int32) for i in range(nq)])
    ki_arr = np.concatenate([np.arange(i + 1, dtype=np.int32) for i in range(nq)])
    n_pairs = qi_arr.shape[0]
    grid_spec = pltpu.PrefetchScalarGridSpec(
        num_scalar_prefetch=2,
        grid=(B, H // nh, n_pairs),
        in_specs=[pl.BlockSpec((1, nh, t, LANES), lambda b, hp, p, qi, ki: (b, hp, qi[p], 0)),
                  pl.BlockSpec((1, nh, t, LANES), lambda b, hp, p, qi, ki: (b, hp, ki[p], 0)),
                  pl.BlockSpec((1, nh, t, LANES), lambda b, hp, p, qi, ki: (b, hp, ki[p], 0))],
        out_specs=pl.BlockSpec((1, t, nh * HEAD_DIM), lambda b, hp, p, qi, ki: (b, qi[p], hp)),
        scratch_shapes=[pltpu.VMEM((nh, t, LANES), F32), pltpu.VMEM((nh, t, LANES), F32)],
    )
    return pl.pallas_call(
        _fox_kernel,
        grid_spec=grid_spec,
        out_shape=jax.ShapeDtypeStruct((B, S, FOX_W), BF16),
        compiler_params=_cparams(("arbitrary", "arbitrary", "arbitrary")),
        name="fox_attn",
    )(jnp.asarray(qi_arr), jnp.asarray(ki_arr), qa, ka, va)


_RV_W0, _RV_A0, _RV_KK, _RV_KA, _RV_RK, _RV_LNG, _RV_LNB = range(7)


def _head_stack(x, n_heads):
    lane = _iota(x.shape, 1)
    return jnp.concatenate(
        [jnp.where(lane // HEAD_DIM == h, x, 0.0) for h in range(n_heads)], axis=0)


def _rwkv_kernel(p_ref, mu_ref, vec_ref, w2_ref, a2_ref, g2_ref, lt_ref, bo_ref, o_ref,
                 prev_ref, s_ref, at_ref, rt_ref, bt_ref, kt_ref, v_ref, wl_ref, y_ref, bon_ref, g_ref,
                 *chunk_bufs):
    nb, ts, _ = p_ref.shape
    C = RWKV_C
    H = RWKV_HEADS
    W = RWKV_W

    @pl.when(pl.program_id(0) == 0)
    def _():
        prev_ref[...] = jnp.zeros_like(prev_ref)
        s_ref[...] = jnp.zeros_like(s_ref)

    vec = vec_ref[...]
    row = lambda i: vec[i:i + 1, :]
    bo = bo_ref[...]

    for b in range(nb):
        p = p_ref[b]
        rowi = _iota(p.shape, 0)
        prev = jnp.where(rowi == 0, jnp.broadcast_to(prev_ref[b, 0:1, :], p.shape), pltpu.roll(p, 1, 0))
        prev_ref[b] = jnp.broadcast_to(p[ts - 1:ts, :], prev_ref.shape[1:])
        xm = p + (prev - p) * mu_ref[...]
        r, k, v = xm[:, 0:W], xm[:, W:2 * W], xm[:, 2 * W:3 * W]
        wd = xm[:, 3 * W:3 * W + W_LORA]
        ad = xm[:, 3 * W + W_LORA:3 * W + W_LORA + A_LORA]
        gd = xm[:, 3 * W + W_LORA + A_LORA:]
        lw = row(_RV_W0) + _dot3(jnp.tanh(wd), w2_ref[...])
        logdec = -jnp.exp(_log_sigmoid(lw) - 0.5)
        a = _sigmoid(row(_RV_A0) + _dot3(ad, a2_ref[...]))
        g_ref[b] = _dot3(_sigmoid(gd), g2_ref[...])
        kk = k * row(_RV_KK)
        kk = kk / jnp.maximum(jnp.sqrt(_dot_01(kk * kk, bo, 2)), 1e-12)
        k2 = k * (1.0 + (a - 1.0) * row(_RV_KA))
        cum = _dot_01(logdec, lt_ref[...], 3, left=True)
        wincl = jnp.exp(cum)
        winv = jnp.exp(-cum)
        at_ref[b] = -kk * jnp.exp(cum - logdec)
        rt_ref[b] = r * wincl
        bt_ref[b] = kk * a * winv
        kt_ref[b] = k2 * winv
        v_ref[b] = v
        wl_ref[b] = wincl
        bon_ref[b] = r * k2 * row(_RV_RK)

    li = _iota((C, H * C), 1) % C
    ri = _iota((C, H * C), 0)
    strict = li < ri
    incl = li <= ri
    bd = (_iota((W, W), 0) // HEAD_DIM) == (_iota((W, W), 1) // HEAD_DIM)
    n_chunks = ts // C

    def prep(ci, bufs):
        pab_ref, prb_ref, base_ref, y0_ref = bufs
        sl = pl.ds(pl.multiple_of(ci * C, C), C)
        for b in range(nb):
            at, rt, bt, kt, vv = at_ref[b, sl, :], rt_ref[b, sl, :], bt_ref[b, sl, :], kt_ref[b, sl, :], v_ref[b, sl, :]
            bstack, kstack, vstack = _head_stack(bt, H), _head_stack(kt, H), _head_stack(vv, H)
            pab_ref[b] = jnp.where(strict, _dot3(at, bstack, nt=True), 0.0)
            pak = jnp.where(strict, _dot3(at, kstack, nt=True), 0.0)
            prb_ref[b] = jnp.where(incl, _dot3(rt, bstack, nt=True), 0.0)
            prk = jnp.where(incl, _dot3(rt, kstack, nt=True), 0.0)
            base_ref[b] = _dot3(pak, vstack)
            y0_ref[b] = _dot3(prk, vstack)

    def seq(ci, bufs):
        pab_ref, prb_ref, base_ref, y0_ref = bufs
        c0 = pl.multiple_of(ci * C, C)
        sl = pl.ds(c0, C)
        s0 = [s_ref[b] for b in range(nb)]
        ar = [_dot3(jnp.concatenate([at_ref[b, sl, :], rt_ref[b, sl, :]], axis=0), s0[b], nt=True)
              for b in range(nb)]
        half = C // 2
        lo_lanes = _iota((8, LANES), 1) < HEAD_DIM
        lo_half = _iota((half, LANES), 1) < HEAD_DIM
        u = {}
        for b in range(nb):
            base = base_ref[b] + ar[b][0:C]
            for hp in range(H // 2):
                u[b, hp] = [base[8 * j:8 * j + 8, hp * LANES:(hp + 1) * LANES] for j in range(C // 8)]
        pab = {(b, hp): pab_ref[b, :, hp * LANES:(hp + 1) * LANES] for b in range(nb) for hp in range(H // 2)}

        def solve_half(t0):
            for t in range(t0, t0 + half - 1):
                for key, w in u.items():
                    u_t = jnp.broadcast_to(w[t // 8][t % 8:t % 8 + 1, :], (8, LANES))
                    for j in range((t + 1) // 8, (t0 + half) // 8):
                        rows = pab[key][8 * j:8 * j + 8, :]
                        coef = jnp.where(lo_lanes, jnp.broadcast_to(rows[:, t:t + 1], (8, LANES)),
                                         jnp.broadcast_to(rows[:, HEAD_DIM + t:HEAD_DIM + t + 1], (8, LANES)))
                        w[j] = w[j] + coef * u_t

        solve_half(0)
        for key, w in u.items():
            first = jnp.concatenate(w[0:half // 8], axis=0)
            a_lo = pab[key][half:C, 0:half]
            a_hi = pab[key][half:C, HEAD_DIM:HEAD_DIM + half]
            inc = jnp.where(lo_half, _dot3(a_lo, first), _dot3(a_hi, first))
            for j in range(half // 8, C // 8):
                w[j] = w[j] + inc[8 * j - half:8 * j - half + 8, :]
        solve_half(half)
        for b in range(nb):
            u_full = jnp.concatenate([jnp.concatenate(u[b, hp], axis=0) for hp in range(H // 2)], axis=1)
            y_ref[b, sl, :] = y0_ref[b] + ar[b][C:2 * C] + _dot3(prb_ref[b], _head_stack(u_full, H))
            upd = _dot3(jnp.concatenate([u_full, v_ref[b, sl, :]], axis=0).T,
                        jnp.concatenate([bt_ref[b, sl, :], kt_ref[b, sl, :]], axis=0))
            wl = wl_ref[b, pl.ds(c0 + C - 1, 1), :]
            s_ref[b] = (s0[b] + jnp.where(bd, upd, 0.0)) * wl

    bufs_a, bufs_b = chunk_bufs[0:4], chunk_bufs[4:8]
    prep(0, bufs_a)

    def pair_body(k, carry):
        ci = 2 * k
        prep(ci + 1, bufs_b)
        seq(ci, bufs_a)
        prep(jnp.minimum(ci + 2, n_chunks - 1), bufs_a)
        seq(ci + 1, bufs_b)
        return carry

    lax.fori_loop(0, n_chunks // 2, pair_body, 0)

    for b in range(nb):
        y = y_ref[b]
        mean = _dot_01(y, bo, 2) * (1.0 / HEAD_DIM)
        d = y - mean
        var = _dot_01(d * d, bo, 2) * (1.0 / HEAD_DIM)
        yn = d * lax.rsqrt(var + RWKV_GN_EPS) * row(_RV_LNG) + row(_RV_LNB)
        bonus = _dot_01(bon_ref[b], bo, 2) * v_ref[b]
        o_ref[b] = (yn + bonus) * g_ref[b]


def _rwkv(rw, prm, l):
    B, S, _ = rw.shape
    ts = RWKV_TS
    full = lambda a: pl.BlockSpec(a.shape, lambda s: (0,) * a.ndim)
    args = [prm["rwkv_mu"][l], prm["rwkv_vec"][l], prm["rwkv_w2"][l], prm["rwkv_a2"][l], prm["rwkv_g2"][l],
            prm["rwkv_lt"], prm["block_ones"]]
    big = lambda: pltpu.VMEM((B, ts, RWKV_W), F32)
    return pl.pallas_call(
        _rwkv_kernel,
        grid=(S // ts,),
        in_specs=[pl.BlockSpec((B, ts, 1024), lambda s: (0, s, 0))] + [full(a) for a in args],
        out_specs=pl.BlockSpec((B, ts, RWKV_W), lambda s: (0, s, 0)),
        out_shape=jax.ShapeDtypeStruct((B, S, RWKV_W), F32),
        scratch_shapes=[pltpu.VMEM((B, 8, 1024), F32), pltpu.VMEM((B, RWKV_W, RWKV_W), F32)]
                       + [big() for _ in range(9)]
                       + [pltpu.VMEM((B, RWKV_C, RWKV_W), F32) for _ in range(8)],
        compiler_params=_cparams(("arbitrary",)),
        name="rwkv",
    )(rw, *args)


def _ret_kernel(p_ref, cos_ref, sin_ref, dm_ref, qd_ref, kd_ref, dmat_ref, bo_ref, gn_ref, o_ref, r_ref):
    C = p_ref.shape[1]
    W = RET_W
    H = RET_HEADS

    @pl.when(pl.program_id(1) == 0)
    def _():
        r_ref[...] = jnp.zeros_like(r_ref)

    p = p_ref[0]
    cos, sin = cos_ref[...], sin_ref[...]
    first_half = (_iota((C, W), 1) % HEAD_DIM) < (HEAD_DIM // 2)

    def rope(x):
        partner = jnp.where(first_half, pltpu.roll(x, W - HEAD_DIM // 2, 1), pltpu.roll(x, HEAD_DIM // 2, 1))
        return x * cos + partner * sin

    q = rope(p[:, 0:W])
    k = rope(p[:, W:2 * W]) * (HEAD_DIM ** -0.5)
    v = p[:, 2 * W:3 * W]
    g = p[:, 3 * W:4 * W]
    state = r_ref[...]
    scores = _dot_nt(q, _head_stack(k, H)) * dm_ref[...]
    y = _dot(scores, _head_stack(v, H)) + _dot(q * qd_ref[...], state)
    kv = _dot((k * kd_ref[...]).T, v)
    bd = (_iota((W, W), 0) // HEAD_DIM) == (_iota((W, W), 1) // HEAD_DIM)
    r_ref[...] = state * dmat_ref[...] + jnp.where(bd, kv, 0.0)

    bo = bo_ref[...]
    mean = _dot_01(y, bo, 2) * (1.0 / HEAD_DIM)
    d = y - mean
    var = _dot_01(d * d, bo, 2) * (1.0 / HEAD_DIM)
    o_ref[0] =d * lax.rsqrt(var + RET_GN_EPS) * gn_ref[...] * (g * _sigmoid(g))


def _retention(rt, prm, l):
    B, S, _ = rt.shape
    C = RET_C
    cst = lambda a: pl.BlockSpec(a.shape, lambda b, s: (0,) * a.ndim)
    consts = [prm["ret_dmask"], prm["ret_qdec"], prm["ret_kdec"], prm["ret_dmat"], prm["block_ones"],
              prm["ret_gn"][l]]
    return pl.pallas_call(
        _ret_kernel,
        grid=(B, S // C),
        in_specs=[pl.BlockSpec((1, C, 1024), lambda b, s: (b, s, 0)),
                  pl.BlockSpec((C, RET_W), lambda b, s: (s, 0)),
                  pl.BlockSpec((C, RET_W), lambda b, s: (s, 0))] + [cst(a) for a in consts],
        out_specs=pl.BlockSpec((1, C, RET_W), lambda b, s: (b, s, 0)),
        out_shape=jax.ShapeDtypeStruct((B, S, RET_W), F32),
        scratch_shapes=[pltpu.VMEM((RET_W, RET_W), F32)],
        compiler_params=_cparams(("arbitrary", "arbitrary")),
        name="retention",
    )(rt, prm["ret_cos"], prm["ret_sin"], *consts)


def _ret_tables(S):
    C, H, d = RET_C, RET_HEADS, HEAD_DIM
    half = d // 2
    inv = ROPE_BASE ** (-jnp.arange(half, dtype=F32) / half)
    ang = jnp.arange(S, dtype=F32)[:, None] * inv[None, :]
    cos, sin = jnp.cos(ang), jnp.sin(ang)
    cos_t = jnp.tile(jnp.concatenate([cos, cos], axis=1), (1, H))
    sin_t = jnp.tile(jnp.concatenate([-sin, sin], axis=1), (1, H))
    log_gamma = jnp.log1p(-jnp.exp2(-5.0 - jnp.arange(H, dtype=F32)))
    idx = jnp.arange(C, dtype=F32)
    diff = idx[:, None] - idx[None, :]
    dmask = jnp.where(diff >= 0, jnp.exp(jnp.maximum(diff, 0.0) * log_gamma[:, None, None]), 0.0)
    dmask = jnp.transpose(dmask, (1, 0, 2)).reshape(C, H * C)
    rep = lambda a: jnp.repeat(a, d, axis=1)
    kdec = rep(jnp.exp((C - 1.0 - idx)[:, None] * log_gamma[None, :]))
    qdec = rep(jnp.exp((idx + 1.0)[:, None] * log_gamma[None, :]))
    hid = np.arange(H * d) // d
    bd = jnp.asarray(hid[:, None] == hid[None, :])
    dmat = jnp.where(bd, jnp.repeat(jnp.exp(C * log_gamma), d)[:, None], 0.0)
    return {"ret_cos": cos_t, "ret_sin": sin_t, "ret_dmask": dmask, "ret_qdec": qdec, "ret_kdec": kdec,
            "ret_dmat": dmat}


def _layer_norm(z, g, b):
    mu = jnp.mean(z, axis=1, keepdims=True)
    d = z - mu
    var = jnp.mean(d * d, axis=1, keepdims=True)
    return d * lax.rsqrt(var + LN_EPS) * g + b


ROW_TILES = D_MODEL // LANES


def _out_kernel(yf_ref, yr_ref, yt_ref, x_ref, w_ref, g_ref, b_ref, wr_ref, br_ref,
                x1_ref, info_ref, info_t_ref):
    tm = x_ref.shape[0]
    mixed = (_dot(yf_ref[...], w_ref[0:FOX_W, :])
             + _dot(yr_ref[...].astype(BF16), w_ref[FOX_W:FOX_W + RWKV_W, :])
             + _dot(yt_ref[...].astype(BF16), w_ref[FOX_W + RWKV_W:, :]))
    x1 = _layer_norm(ALPHA * x_ref[...] + mixed, g_ref[...], b_ref[...])
    x1_ref[...] = x1

    logits = _dot3(x1, wr_ref[...]) + br_ref[...]
    lane = _iota((tm, LANES), 1).astype(F32)
    ninf = -jnp.inf
    first = lambda hit: jnp.min(jnp.where(hit, lane, float(LANES)), axis=1, keepdims=True)
    gl = jnp.where(lane < N_GROUPS, logits, ninf)
    gmax = jnp.max(gl, axis=1, keepdims=True)
    grp = first(gl == gmax)
    g_gate = 1.0 / jnp.sum(jnp.exp(gl - gmax), axis=1, keepdims=True)
    lo = N_GROUPS + grp * EXPERTS_PER_GROUP
    el = jnp.where(jnp.logical_and(lane >= lo, lane < lo + EXPERTS_PER_GROUP), logits, ninf)
    v1 = jnp.max(el, axis=1, keepdims=True)
    i1 = first(el == v1)
    el2 = jnp.where(lane == i1, ninf, el)
    v2 = jnp.max(el2, axis=1, keepdims=True)
    i2 = first(el2 == v2)
    e = jnp.exp(v2 - v1)
    den = 1.0 / (1.0 + e)
    info = jnp.where(lane == 0, i1 - N_GROUPS, jnp.where(lane == 1, i2 - N_GROUPS,
           jnp.where(lane == 2, g_gate * den, jnp.where(lane == 3, g_gate * e * den, 0.0))))
    info_ref[...] = info
    info_t_ref[...] = info.T[0:8, :]


def _out_proj(yf, yr, yt, x, prm, l):
    T = x.shape[0]
    tm = OUT_TM
    rows = lambda w: pl.BlockSpec((tm, w), lambda i: (i, 0))
    cst = lambda a: pl.BlockSpec(a.shape, lambda i: (0,) * a.ndim)
    consts = [prm["w_out"][l], prm["ln1_g"][l], prm["ln1_b"][l], prm["w_router"][l], prm["b_router"][l]]
    return pl.pallas_call(
        _out_kernel,
        grid=(T // tm,),
        in_specs=[rows(FOX_W), rows(RWKV_W), rows(RET_W), rows(D_MODEL)] + [cst(a) for a in consts],
        out_specs=[rows(D_MODEL), rows(LANES), pl.BlockSpec((8, tm), lambda i: (0, i))],
        out_shape=[jax.ShapeDtypeStruct((T, D_MODEL), F32), jax.ShapeDtypeStruct((T, LANES), F32),
                   jax.ShapeDtypeStruct((8, T), F32)],
        compiler_params=_cparams(("arbitrary",)),
        name="out_proj",
    )(yf, yr, yt, x, *consts)


def _rank_kernel(it_ref, tri_ref, rk_ref, cnt_ref, carry_ref):
    tm = it_ref.shape[1]

    @pl.when(pl.program_id(0) == 0)
    def _():
        carry_ref[...] = jnp.zeros_like(carry_ref)

    sub = _iota((N_EXPERTS, tm), 0).astype(F32)
    oh1 = sub == it_ref[0:1, :]
    oh2 = sub == it_ref[1:2, :]
    oh = jnp.where(jnp.logical_or(oh1, oh2), 1.0, 0.0)
    before = _dot(oh.astype(BF16), tri_ref[...]) + carry_ref[:, 0:1]
    r1 = jnp.sum(jnp.where(oh1, before, 0.0), axis=0, keepdims=True)
    r2 = jnp.sum(jnp.where(oh2, before, 0.0), axis=0, keepdims=True)
    rk_ref[...] = jnp.concatenate([r1, r2, jnp.zeros((6, tm), F32)], axis=0)
    carry_ref[...] = carry_ref[...] + jnp.sum(oh, axis=1, keepdims=True)
    cnt_ref[...] = carry_ref[...]


def _rank(info_t, tri):
    T = info_t.shape[1]
    tm = RANK_TM
    return pl.pallas_call(
        _rank_kernel,
        grid=(T // tm,),
        in_specs=[pl.BlockSpec((8, tm), lambda i: (0, i)), pl.BlockSpec((tm, tm), lambda i: (0, 0))],
        out_specs=[pl.BlockSpec((8, tm), lambda i: (0, i)), pl.BlockSpec((N_EXPERTS, LANES), lambda i: (0, 0))],
        out_shape=[jax.ShapeDtypeStruct((8, T), F32), jax.ShapeDtypeStruct((N_EXPERTS, LANES), F32)],
        scratch_shapes=[pltpu.VMEM((N_EXPERTS, LANES), F32)],
        compiler_params=_cparams(("arbitrary",)),
        name="rank",
    )(info_t, tri)


def _row_copy(src_hbm, row, dst, r, sem):
    return pltpu.make_async_copy(src_hbm.at[row], dst.at[:, r, :], sem)


def _expert_kernel(be_ref, nu_ref, src0_ref, src1_ref, x_hbm, w1_ref, w3_ref, w2_ref, y_ref,
                   xbuf, sem, w1b, w3b, w2b):
    j = pl.program_id(0)
    tm = xbuf.shape[2]
    nu = nu_ref[0]
    slot = j % 2

    def gather(src_ref, s):
        for r in range(tm):
            _row_copy(x_hbm, src_ref[0, 0, r], xbuf.at[s], r, sem.at[s]).start(priority=r % 2)

    def drain(s):
        def body(r, c):
            _row_copy(x_hbm, 0, xbuf.at[s], r, sem.at[s]).wait()
            return c
        lax.fori_loop(0, tm, body, 0, unroll=8)

    @pl.when(j == 0)
    def _():
        gather(src0_ref, 0)

    @pl.when(j < nu)
    def _():
        drain(slot)
        gather(src1_ref, 1 - slot)

    @pl.when(jnp.logical_and(j < nu, jnp.logical_or(j == 0, be_ref[j] != be_ref[jnp.maximum(j - 1, 0)])))
    def _():
        w1b[...] = w1_ref[0].astype(BF16)
        w3b[...] = w3_ref[0].astype(BF16)
        w2b[...] = w2_ref[0].astype(BF16)

    @pl.when(j < nu)
    def _():
        x = jnp.concatenate([xbuf[slot, c] for c in range(ROW_TILES)], axis=1).astype(BF16)
        a = _dot(x, w1b[...])
        h = (a * _sigmoid(a)) * _dot(x, w3b[...])
        y = _dot(h.astype(BF16), w2b[...])
        for c in range(ROW_TILES):
            y_ref[:, c, :] = y[:, c * LANES:(c + 1) * LANES]

    @pl.when(j == nu)
    def _():
        drain(slot)

    @pl.when(j >= nu)
    def _():
        y_ref[...] = jnp.zeros_like(y_ref)


def _experts(block_expert, n_used, src_tok, x1, w1, w3, w2):
    n_blk = block_expert.shape[0]
    tm = MOE_TM
    src_tok = src_tok.reshape(n_blk, 1, tm)
    grid_spec = pltpu.PrefetchScalarGridSpec(
        num_scalar_prefetch=2,
        grid=(n_blk,),
        in_specs=[pl.BlockSpec((1, 1, tm), lambda j, be, nu: (0, 0, 0), memory_space=pltpu.SMEM),
                  pl.BlockSpec((1, 1, tm), lambda j, be, nu: (jnp.minimum(j + 1, n_blk - 1), 0, 0),
                               memory_space=pltpu.SMEM),
                  pl.BlockSpec(memory_space=pl.ANY),
                  pl.BlockSpec((1, D_MODEL, D_EXPERT), lambda j, be, nu: (be[j], 0, 0)),
                  pl.BlockSpec((1, D_MODEL, D_EXPERT), lambda j, be, nu: (be[j], 0, 0)),
                  pl.BlockSpec((1, D_EXPERT, D_MODEL), lambda j, be, nu: (be[j], 0, 0))],
        out_specs=pl.BlockSpec((tm, ROW_TILES, LANES), lambda j, be, nu: (j, 0, 0)),
        scratch_shapes=[pltpu.VMEM((2, ROW_TILES, tm, LANES), F32), pltpu.SemaphoreType.DMA((2,)),
                        pltpu.VMEM((D_MODEL, D_EXPERT), BF16), pltpu.VMEM((D_MODEL, D_EXPERT), BF16),
                        pltpu.VMEM((D_EXPERT, D_MODEL), BF16)],
    )
    return pl.pallas_call(
        _expert_kernel,
        grid_spec=grid_spec,
        out_shape=jax.ShapeDtypeStruct((n_blk * tm, ROW_TILES, LANES), F32),
        compiler_params=_cparams(("arbitrary",)),
        name="experts",
    )(block_expert, n_used, src_tok, src_tok, x1, w1, w3, w2)


def _combine_kernel(dest0_ref, dest1_ref, y_hbm, info_ref, x_ref, g_ref, b_ref, o_ref, ybuf, sem):
    i = pl.program_id(0)
    tm = x_ref.shape[0]
    slot = i % 2

    def gather(dest_ref, s):
        for r in range(2 * tm):
            _row_copy(y_hbm, dest_ref[0, 0, r], ybuf.at[s], r, sem.at[s]).start(priority=r % 2)

    @pl.when(i == 0)
    def _():
        gather(dest0_ref, 0)

    @pl.when(i + 1 < pl.num_programs(0))
    def _():
        gather(dest1_ref, 1 - slot)

    def drain(r, c):
        _row_copy(y_hbm, 0, ybuf.at[slot], r, sem.at[slot]).wait()
        return c
    lax.fori_loop(0, 2 * tm, drain, 0, unroll=8)
    info = info_ref[...]
    rows = lambda lo: jnp.concatenate([ybuf[slot, c, pl.ds(lo, tm), :] for c in range(ROW_TILES)], axis=1)
    moe = info[:, 2:3] * rows(0) + info[:, 3:4] * rows(tm)
    o_ref[...] = _layer_norm(ALPHA * x_ref[...] + moe, g_ref[...], b_ref[...])


def _combine(dest, y_pad, info, x1, g, b):
    T = x1.shape[0]
    tm = COMB_TM
    cst = lambda a: pl.BlockSpec(a.shape, lambda i: (0,) * a.ndim)
    n = T // tm
    return pl.pallas_call(
        _combine_kernel,
        grid=(n,),
        in_specs=[pl.BlockSpec((1, 1, 2 * tm), lambda i: (0, 0, 0), memory_space=pltpu.SMEM),
                  pl.BlockSpec((1, 1, 2 * tm), lambda i: (jnp.minimum(i + 1, n - 1), 0, 0),
                               memory_space=pltpu.SMEM),
                  pl.BlockSpec(memory_space=pl.ANY),
                  pl.BlockSpec((tm, LANES), lambda i: (i, 0)),
                  pl.BlockSpec((tm, D_MODEL), lambda i: (i, 0)), cst(g), cst(b)],
        out_specs=pl.BlockSpec((tm, D_MODEL), lambda i: (i, 0)),
        out_shape=jax.ShapeDtypeStruct((T, D_MODEL), F32),
        scratch_shapes=[pltpu.VMEM((2, ROW_TILES, 2 * tm, LANES), F32), pltpu.SemaphoreType.DMA((2,))],
        compiler_params=_cparams(("arbitrary",)),
        name="combine",
    )(dest, dest, y_pad, info, x1, g, b)


def _moe(x1, x1t, info, info_t, prm, l):
    T = x1.shape[0]
    tm = MOE_TM
    n_blk = (2 * T) // tm + N_EXPERTS
    ranks, cnt = _rank(info_t, prm["tri_rank"])
    counts = cnt[:, 0].astype(jnp.int32)
    padded = ((counts + tm - 1) // tm) * tm
    pad_end = jnp.cumsum(padded)
    pad_off = pad_end - padded
    eid = info_t[0:2].astype(jnp.int32)
    e_iota = jnp.arange(N_EXPERTS, dtype=jnp.int32)
    off = jnp.sum(jnp.where(eid[:, None, :] == e_iota[None, :, None], pad_off[None, :, None], 0), axis=1)
    dest = off + ranks[0:2].astype(jnp.int32)
    tok = jnp.broadcast_to(jnp.arange(T, dtype=jnp.int32), (2, T))
    src_tok = jnp.zeros((n_blk * tm,), jnp.int32).at[dest.reshape(-1)].set(tok.reshape(-1))
    blk_start = jnp.arange(n_blk, dtype=jnp.int32) * tm
    block_expert = jnp.minimum(jnp.sum((pad_end[None, :] <= blk_start[:, None]).astype(jnp.int32), axis=1),
                               N_EXPERTS - 1)
    n_used = (pad_end[-1] // tm).astype(jnp.int32).reshape(1)
    block_expert = jnp.where(jnp.arange(n_blk) < n_used[0], block_expert,
                             block_expert[jnp.maximum(n_used[0] - 1, 0)])
    y_pad = _experts(block_expert, n_used, src_tok, x1t, prm["moe_w1"][l], prm["moe_w3"][l], prm["moe_w2"][l])
    ct = COMB_TM
    dest_blk = dest.reshape(2, T // ct, ct).transpose(1, 0, 2).reshape(T // ct, 1, 2 * ct)
    return _combine(dest_blk, y_pad, info, x1, prm["ln2_g"][l], prm["ln2_b"][l])


def _prep_params(p):
    L = p["w_in"].shape[0]
    w = p["w_in"]
    fg = jnp.pad(w[:, :, 1536:1544], ((0, 0), (0, 0), (0, LANES - FOX_HEADS)))
    w_in = jnp.concatenate([w[:, :, 0:1536], fg, w[:, :, 1544:2568], w[:, :, 2568:3592]], axis=-1)
    out = {
        "w_in": w_in.astype(BF16),
        "fbias": jnp.pad(p["fox_fgate_bias"], ((0, 0), (0, LANES - FOX_HEADS)))[:, None, :],
        "tri_proj": jnp.tril(jnp.ones((PROJ_TM, PROJ_TM), BF16)),
    }
    vec = jnp.stack([p["rwkv_w0"], p["rwkv_a0"], p["rwkv_k_k"], p["rwkv_k_a"],
                     p["rwkv_r_k"].reshape(L, RWKV_W), p["rwkv_ln_g"], p["rwkv_ln_b"],
                     jnp.zeros((L, RWKV_W), F32)], axis=1)
    ti = np.arange(RWKV_TS)
    lt = ((ti[:, None] // RWKV_C == ti[None, :] // RWKV_C) & (ti[None, :] <= ti[:, None])).astype(np.float32)
    hi = np.arange(RWKV_W) // HEAD_DIM
    out.update({
        "rwkv_mu": jnp.concatenate([p["rwkv_mu_rkv"].reshape(L, 3 * RWKV_W), p["rwkv_mu_lora"]], axis=-1)[:, None, :],
        "rwkv_vec": vec,
        "rwkv_w2": p["rwkv_w2"], "rwkv_a2": p["rwkv_a2"], "rwkv_g2": p["rwkv_g2"],
        "rwkv_lt": jnp.asarray(lt, dtype=BF16),
        "block_ones": jnp.asarray((hi[:, None] == hi[None, :]).astype(np.float32), dtype=BF16),
    })
    out.update(_ret_tables(p["x"].shape[1]))
    out["ret_gn"] = p["ret_gn_g"][:, None, :]
    pad_r = LANES - N_GROUPS - N_EXPERTS
    ri = np.arange(RANK_TM)
    out.update({
        "w_out": p["w_out"].astype(BF16),
        "ln1_g": p["ln1_g"][:, None, :], "ln1_b": p["ln1_b"][:, None, :],
        "ln2_g": p["ln2_g"][:, None, :], "ln2_b": p["ln2_b"][:, None, :],
        "w_router": jnp.pad(jnp.concatenate([p["moe_w_group"], p["moe_w_expert"]], axis=-1),
                            ((0, 0), (0, 0), (0, pad_r))),
        "b_router": jnp.pad(jnp.concatenate([p["moe_b_group"], p["moe_b_expert"]], axis=-1),
                            ((0, 0), (0, pad_r)))[:, None, :],
        "tri_rank": jnp.asarray((ri[:, None] < ri[None, :]).astype(np.float32), dtype=BF16),
        "moe_w1": p["moe_w1"], "moe_w3": p["moe_w3"], "moe_w2": p["moe_w2"],
    })
    return out


def kernel(x, w_in, fox_fgate_bias, rwkv_mu_rkv, rwkv_mu_lora, rwkv_w0, rwkv_w2, rwkv_a0, rwkv_a2, rwkv_g2,
           rwkv_k_k, rwkv_k_a, rwkv_r_k, rwkv_ln_g, rwkv_ln_b, ret_gn_g, w_out, ln1_g, ln1_b, ln2_g, ln2_b,
           moe_w_group, moe_b_group, moe_w_expert, moe_b_expert, moe_w1, moe_w3, moe_w2):
    prm = _prep_params(dict(
        x=x, w_in=w_in, fox_fgate_bias=fox_fgate_bias, rwkv_mu_rkv=rwkv_mu_rkv, rwkv_mu_lora=rwkv_mu_lora,
        rwkv_w0=rwkv_w0, rwkv_w2=rwkv_w2, rwkv_a0=rwkv_a0, rwkv_a2=rwkv_a2, rwkv_g2=rwkv_g2,
        rwkv_k_k=rwkv_k_k, rwkv_k_a=rwkv_k_a, rwkv_r_k=rwkv_r_k, rwkv_ln_g=rwkv_ln_g, rwkv_ln_b=rwkv_ln_b,
        ret_gn_g=ret_gn_g, w_out=w_out, ln1_g=ln1_g, ln1_b=ln1_b, ln2_g=ln2_g, ln2_b=ln2_b,
        moe_w_group=moe_w_group, moe_b_group=moe_b_group, moe_w_expert=moe_w_expert,
        moe_b_expert=moe_b_expert, moe_w1=moe_w1, moe_w3=moe_w3, moe_w2=moe_w2))
    B, S, D = x.shape
    T = B * S
    for l in range(w_in.shape[0]):
        qa, ka, va, rw, rt = _project(x, prm["w_in"][l], prm["fbias"][l], prm["tri_proj"])
        y_fox = _fox_attention(qa, ka, va)
        y_rwkv = _rwkv(rw, prm, l)
        y_ret = _retention(rt, prm, l)
        x1, info, info_t = _out_proj(y_fox.reshape(T, FOX_W), y_rwkv.reshape(T, RWKV_W),
                                     y_ret.reshape(T, RET_W), x.reshape(T, D), prm, l)
        x = _moe(x1, x1.reshape(T, ROW_TILES, LANES), info, info_t, prm, l).reshape(B, S, D)
    return x
```

```python
import functools
import math

import jax
import jax.numpy as jnp
import numpy as np
from jax import lax
from jax.experimental import pallas as pl
from jax.experimental.pallas import tpu as pltpu

F32 = jnp.float32
BF16 = jnp.bfloat16
HIGHEST = lax.Precision.HIGHEST

D_MODEL = 1024
DEPTH = 4
HEAD_DIM = 64
FOX_W, RWKV_W, RET_W = 512, 256, 256
FOX_HEADS, RWKV_HEADS, RET_HEADS = 8, 4, 4
W_LORA, A_LORA, G_LORA = 64, 64, 128
ROPE_BASE = 10000.0
N_GROUPS, EXPERTS_PER_GROUP, N_EXPERTS, D_EXPERT = 4, 8, 32, 512
ALPHA = (2.0 * DEPTH) ** 0.25
LN_EPS = 1e-5
LOG2E = math.log2(math.e)
RWKV_GN_EPS = 64e-5
RET_GN_EPS = 1e-6

LANES = 128
VMEM_LIMIT = 56 * 1024 * 1024

PROJ_TM = 512
ATT_T = 1024
ATT_HEADS = 4
ATT_SKIP_MARGIN = 160.0
RWKV_TS = 512
RWKV_C = 64
RET_C = 256
OUT_TM = 512
RANK_TM = 512
MOE_TM = 256
COMB_TM = 256

P_FOX = 0
P_FG = 1536
P_RWKV = 1664
P_RET = 2688
P_TOT = 3712


def _cparams(sem):
    return pltpu.CompilerParams(dimension_semantics=sem, vmem_limit_bytes=VMEM_LIMIT)


def _dot(a, b, precision=None):
    return jnp.dot(a, b, preferred_element_type=F32, precision=precision)


def _dot_nt(a, b, precision=None):
    return lax.dot_general(a, b, (((1,), (1,)), ((), ())), preferred_element_type=F32,
                           precision=precision)


def _split(a, n):
    parts, r = [], a
    for i in range(n):
        h = r.astype(BF16)
        parts.append(h)
        if i + 1 < n:
            r = r - h.astype(F32)
    return parts


def _dot3(a, b, nt=False):
    d = _dot_nt if nt else _dot
    a1, a2 = _split(a, 2)
    b1, b2 = _split(b, 2)
    return (d(a1, b2) + d(a2, b1)) + d(a1, b1)


def _dot_01(a, m, n=3, left=False):
    d = (lambda p: _dot(m, p)) if left else (lambda p: _dot(p, m))
    parts = _split(a, n)
    acc = d(parts[-1])
    for p in parts[-2::-1]:
        acc = acc + d(p)
    return acc


def _log_sigmoid(z):
    return -(jnp.maximum(-z, 0.0) + jnp.log(1.0 + jnp.exp(-jnp.abs(z))))


def _sigmoid(z):
    return 1.0 / (1.0 + jnp.exp(-z))


def _iota(shape, dim):
    return lax.broadcasted_iota(jnp.int32, shape, dim)


def _proj_kernel(x_ref, w_ref, fb_ref, tri_ref, qa_ref, ka_ref, va_ref, rw_ref, rt_ref, stat_ref, carry_ref):
    tm = x_ref.shape[1]

    @pl.when(pl.program_id(1) == 0)
    def _():
        carry_ref[...] = jnp.zeros_like(carry_ref)

    xb = x_ref[0].astype(BF16)
    p_fox = _dot(xb, w_ref[:, P_FOX:P_FG])
    z = _dot(xb, w_ref[:, P_FG:P_RWKV]) + fb_ref[...]
    rw_ref[0] = _dot(xb, w_ref[:, P_RWKV:P_RET])
    rt_ref[0] = _dot(xb, w_ref[:, P_RET:P_TOT])

    c = _dot_01(_log_sigmoid(z), tri_ref[...], 3, left=True) + carry_ref[0:1, :]
    carry_ref[...] = jnp.broadcast_to(c[tm - 1:tm, :], carry_ref.shape)

    lane = _iota((tm, LANES), 1)
    for h in range(FOX_HEADS):
        cb = jnp.broadcast_to(c[:, h:h + 1], (tm, LANES)) * LOG2E
        hi = cb.astype(BF16).astype(F32)
        r1 = cb - hi
        mid = r1.astype(BF16).astype(F32)
        lo = r1 - mid
        tile = (h // 2) * LANES

        def head_lanes(base):
            t = p_fox[:, base + tile: base + tile + LANES]
            return pltpu.roll(t, 64, 1) if h % 2 else t

        q = head_lanes(0) * (LOG2E * HEAD_DIM ** -0.5)
        k = head_lanes(FOX_W)
        v = head_lanes(2 * FOX_W)
        qa = jnp.where(lane < 64, q, jnp.where(lane == 64, hi, jnp.where(lane == 65, mid,
             jnp.where(lane == 66, lo, jnp.where(lane < 70, 1.0, 0.0)))))
        ka = jnp.where(lane < 64, k, jnp.where(lane < 67, 1.0, jnp.where(lane == 67, -hi,
             jnp.where(lane == 68, -mid, jnp.where(lane == 69, -lo, 0.0)))))
        va = jnp.where(lane < 64, v, jnp.where(lane == 64, 1.0, 0.0))
        qa, ka = qa.astype(BF16), ka.astype(BF16)
        qa_ref[0, h] = qa
        ka_ref[0, h] = ka
        va_ref[0, h] = va.astype(BF16)
        qr = jnp.where(lane < 64, qa.astype(F32), 0.0)
        kr = jnp.where(lane < 64, ka.astype(F32), 0.0)
        col_max = lambda a: jnp.max(jnp.sum(a, axis=1, keepdims=True), axis=0, keepdims=True)
        l1 = _iota((1, LANES), 1)
        stat_ref[0, 0, h:h + 1, :] = jnp.where(
            l1 == 0, col_max(qr * qr), jnp.where(l1 == 1, col_max(kr * kr), jnp.where(
                l1 == 2, cb[0:1, :], jnp.where(l1 == 3, cb[tm - 1:tm, :], 0.0))))


def _project(x, w_bf16, fbias, tri):
    B, S, D = x.shape
    tm = PROJ_TM
    aug = jax.ShapeDtypeStruct((B, FOX_HEADS, S, LANES), BF16)
    wide = jax.ShapeDtypeStruct((B, S, 1024), F32)
    aug_spec = pl.BlockSpec((1, FOX_HEADS, tm, LANES), lambda b, s: (b, 0, s, 0))
    wide_spec = pl.BlockSpec((1, tm, 1024), lambda b, s: (b, s, 0))
    return pl.pallas_call(
        _proj_kernel,
        grid=(B, S // tm),
        in_specs=[pl.BlockSpec((1, tm, D), lambda b, s: (b, s, 0)),
                  pl.BlockSpec((D, P_TOT), lambda b, s: (0, 0)),
                  pl.BlockSpec((1, LANES), lambda b, s: (0, 0)),
                  pl.BlockSpec((tm, tm), lambda b, s: (0, 0))],
        out_specs=[aug_spec, aug_spec, aug_spec, wide_spec, wide_spec,
                   pl.BlockSpec((1, 1, FOX_HEADS, LANES), lambda b, s: (b, s, 0, 0))],
        out_shape=[aug, aug, aug, wide, wide, jax.ShapeDtypeStruct((B, S // tm, FOX_HEADS, LANES), F32)],
        scratch_shapes=[pltpu.VMEM((8, LANES), F32)],
        compiler_params=_cparams(("arbitrary", "arbitrary")),
        name="proj",
    )(x, w_bf16, fbias, tri)


def _fox_kernel(qi_ref, ki_ref, skip_ref, q_ref, k_ref, v_ref, o_ref, m_ref, acc_ref):
    p = pl.program_id(2)
    qi = qi_ref[p]
    ki = ki_ref[p]
    nh, t = q_ref.shape[1], q_ref.shape[2]
    skip = skip_ref[(pl.program_id(0) * pl.num_programs(1) + pl.program_id(1)) * pl.num_programs(2) + p]

    @pl.when(ki == qi)
    def _():
        m_ref[...] = jnp.full_like(m_ref, -jnp.inf)
        acc_ref[...] = jnp.zeros_like(acc_ref)

    def step(masked):
        qk = lambda j: _dot_nt(q_ref[0, j], k_ref[0, j])
        ss = {j: qk(j) for j in range(min(2, nh))}
        for j in range(nh):
            if j + 2 < nh:
                ss[j + 2] = qk(j + 2)
            s = ss.pop(j)
            if masked:
                s = jnp.where(_iota((t, t), 1) <= _iota((t, t), 0), s, -jnp.inf)
            m_prev = m_ref[j]
            m_new = jnp.maximum(m_prev, jnp.max(s, axis=1, keepdims=True))
            alpha = jnp.exp2(m_prev - m_new)
            pm = jnp.exp2(s - jnp.concatenate([m_new] * (t // LANES), axis=1))
            acc_ref[j] = alpha * acc_ref[j] + _dot(pm.astype(BF16), v_ref[0, j])
            m_ref[j] = m_new

    @pl.when(ki == qi)
    def _():
        step(True)

    @pl.when(jnp.logical_and(ki < qi, skip == 0))
    def _():
        step(False)

    @pl.when(ki == 0)
    def _():
        lane = _iota((t, LANES), 1)
        for jp in range(nh // 2):
            a0, a1 = acc_ref[2 * jp], acc_ref[2 * jp + 1]
            o0, o1 = a0 / a0[:, 64:65], a1 / a1[:, 64:65]
            o_ref[0, :, jp * LANES:(jp + 1) * LANES] = jnp.where(
                lane < 64, o0, pltpu.roll(o1, 64, 1)).astype(o_ref.dtype)


def _fox_skip_flags(stats, qi_arr, ki_arr):
    B, n_sub, H, _ = stats.shape
    r = ATT_T // PROJ_TM
    st = stats.reshape(B, n_sub // r, r, H, LANES)
    qn = jnp.sqrt(jnp.max(st[..., 0], axis=2))
    kn = jnp.sqrt(jnp.max(st[..., 1], axis=2))
    c_first = st[:, :, 0, :, 2]
    c_last = st[:, :, r - 1, :, 3]
    qi, ki = jnp.asarray(qi_arr), jnp.asarray(ki_arr)
    bound = qn[:, qi] * kn[:, ki] + c_first[:, qi] - c_last[:, ki]
    dead = bound < -(qn * kn)[:, qi] - ATT_SKIP_MARGIN
    dead = jnp.all(dead.reshape(B, -1, H // ATT_HEADS, ATT_HEADS), axis=-1)
    return jnp.transpose(dead, (0, 2, 1)).astype(jnp.int32).reshape(-1)


def _fox_attention(qa, ka, va, stats):
    B, H, S, _ = qa.shape
    t = ATT_T
    nh = ATT_HEADS
    nq = S // t
    qi_arr = np.concatenate([np.full(i + 1, i, np.int32) for i in range(nq)])
    ki_arr = np.concatenate([np.arange(i, -1, -1, dtype=np.int32) for i in range(nq)])
    n_pairs = qi_arr.shape[0]
    skip = _fox_skip_flags(stats, qi_arr, ki_arr)
    grid_spec = pltpu.PrefetchScalarGridSpec(
        num_scalar_prefetch=3,
        grid=(B, H // nh, n_pairs),
        in_specs=[pl.BlockSpec((1, nh, t, LANES), lambda b, hp, p, qi, ki, sk: (b, hp, qi[p], 0)),
                  pl.BlockSpec((1, nh, t, LANES), lambda b, hp, p, qi, ki, sk: (b, hp, ki[p], 0)),
                  pl.BlockSpec((1, nh, t, LANES), lambda b, hp, p, qi, ki, sk: (b, hp, ki[p], 0))],
        out_specs=pl.BlockSpec((1, t, nh * HEAD_DIM), lambda b, hp, p, qi, ki, sk: (b, qi[p], hp)),
        scratch_shapes=[pltpu.VMEM((nh, t, LANES), F32), pltpu.VMEM((nh, t, LANES), F32)],
    )
    return pl.pallas_call(
        _fox_kernel,
        grid_spec=grid_spec,
        out_shape=jax.ShapeDtypeStruct((B, S, FOX_W), BF16),
        compiler_params=_cparams(("arbitrary", "arbitrary", "arbitrary")),
        name="fox_attn",
    )(jnp.asarray(qi_arr), jnp.asarray(ki_arr), skip, qa, ka, va)


_RV_W0, _RV_A0, _RV_KK, _RV_KA, _RV_RK, _RV_LNG, _RV_LNB = range(7)


def _head_stack(x, n_heads):
    lane = _iota(x.shape, 1)
    return jnp.concatenate(
        [jnp.where(lane // HEAD_DIM == h, x, 0.0) for h in range(n_heads)], axis=0)


def _rwkv_kernel(p_ref, mu_ref, vec_ref, w2_ref, a2_ref, g2_ref, lt_ref, bo_ref, o_ref,
                 prev_ref, s_ref, at_ref, rt_ref, bt_ref, kt_ref, v_ref, wl_ref, y_ref, bon_ref, g_ref,
                 *chunk_bufs):
    nb, ts, _ = p_ref.shape
    C = RWKV_C
    H = RWKV_HEADS
    W = RWKV_W

    @pl.when(pl.program_id(0) == 0)
    def _():
        prev_ref[...] = jnp.zeros_like(prev_ref)
        s_ref[...] = jnp.zeros_like(s_ref)

    vec = vec_ref[...]
    row = lambda i: vec[i:i + 1, :]
    bo = bo_ref[...]

    for b in range(nb):
        p = p_ref[b]
        rowi = _iota(p.shape, 0)
        prev = jnp.where(rowi == 0, jnp.broadcast_to(prev_ref[b, 0:1, :], p.shape), pltpu.roll(p, 1, 0))
        prev_ref[b] = jnp.broadcast_to(p[ts - 1:ts, :], prev_ref.shape[1:])
        xm = p + (prev - p) * mu_ref[...]
        r, k, v = xm[:, 0:W], xm[:, W:2 * W], xm[:, 2 * W:3 * W]
        wd = xm[:, 3 * W:3 * W + W_LORA]
        ad = xm[:, 3 * W + W_LORA:3 * W + W_LORA + A_LORA]
        gd = xm[:, 3 * W + W_LORA + A_LORA:]
        lw = row(_RV_W0) + _dot3(jnp.tanh(wd), w2_ref[...])
        logdec = -jnp.exp(_log_sigmoid(lw) - 0.5)
        a = _sigmoid(row(_RV_A0) + _dot3(ad, a2_ref[...]))
        g_ref[b] = _dot3(_sigmoid(gd), g2_ref[...])
        kk = k * row(_RV_KK)
        kk = kk / jnp.maximum(jnp.sqrt(_dot_01(kk * kk, bo, 2)), 1e-12)
        k2 = k * (1.0 + (a - 1.0) * row(_RV_KA))
        cum = _dot_01(logdec, lt_ref[...], 3, left=True)
        wincl = jnp.exp(cum)
        winv = jnp.exp(-cum)
        at_ref[b] = -kk * jnp.exp(cum - logdec)
        rt_ref[b] = r * wincl
        bt_ref[b] = kk * a * winv
        kt_ref[b] = k2 * winv
        v_ref[b] = v
        wl_ref[b] = wincl
        bon_ref[b] = r * k2 * row(_RV_RK)

    li = _iota((C, H * C), 1) % C
    ri = _iota((C, H * C), 0)
    strict = li < ri
    incl = li <= ri
    bd = (_iota((W, W), 0) // HEAD_DIM) == (_iota((W, W), 1) // HEAD_DIM)
    n_chunks = ts // C

    def prep(ci, bufs):
        pab_ref, prb_ref, base_ref, y0_ref = bufs
        sl = pl.ds(pl.multiple_of(ci * C, C), C)
        for b in range(nb):
            at, rt, bt, kt, vv = at_ref[b, sl, :], rt_ref[b, sl, :], bt_ref[b, sl, :], kt_ref[b, sl, :], v_ref[b, sl, :]
            bstack, kstack, vstack = _head_stack(bt, H), _head_stack(kt, H), _head_stack(vv, H)
            pab_ref[b] = jnp.where(strict, _dot3(at, bstack, nt=True), 0.0)
            pak = jnp.where(strict, _dot3(at, kstack, nt=True), 0.0)
            prb_ref[b] = jnp.where(incl, _dot3(rt, bstack, nt=True), 0.0)
            prk = jnp.where(incl, _dot3(rt, kstack, nt=True), 0.0)
            base_ref[b] = _dot3(pak, vstack)
            y0_ref[b] = _dot3(prk, vstack)

    def seq(ci, bufs):
        pab_ref, prb_ref, base_ref, y0_ref = bufs
        c0 = pl.multiple_of(ci * C, C)
        sl = pl.ds(c0, C)
        s0 = [s_ref[b] for b in range(nb)]
        ar = [_dot3(jnp.concatenate([at_ref[b, sl, :], rt_ref[b, sl, :]], axis=0), s0[b], nt=True)
              for b in range(nb)]
        half = C // 2
        lo_lanes = _iota((8, LANES), 1) < HEAD_DIM
        lo_half = _iota((half, LANES), 1) < HEAD_DIM
        u = {}
        for b in range(nb):
            base = base_ref[b] + ar[b][0:C]
            for hp in range(H // 2):
                u[b, hp] = [base[8 * j:8 * j + 8, hp * LANES:(hp + 1) * LANES] for j in range(C // 8)]
        pab = {(b, hp): pab_ref[b, :, hp * LANES:(hp + 1) * LANES] for b in range(nb) for hp in range(H // 2)}

        def solve_half(t0):
            for t in range(t0, t0 + half - 1):
                for key, w in u.items():
                    u_t = jnp.broadcast_to(w[t // 8][t % 8:t % 8 + 1, :], (8, LANES))
                    for j in range((t + 1) // 8, (t0 + half) // 8):
                        rows = pab[key][8 * j:8 * j + 8, :]
                        coef = jnp.where(lo_lanes, jnp.broadcast_to(rows[:, t:t + 1], (8, LANES)),
                                         jnp.broadcast_to(rows[:, HEAD_DIM + t:HEAD_DIM + t + 1], (8, LANES)))
                        w[j] = w[j] + coef * u_t

        solve_half(0)
        for key, w in u.items():
            first = jnp.concatenate(w[0:half // 8], axis=0)
            a_lo = pab[key][half:C, 0:half]
            a_hi = pab[key][half:C, HEAD_DIM:HEAD_DIM + half]
            inc = jnp.where(lo_half, _dot3(a_lo, first), _dot3(a_hi, first))
            for j in range(half // 8, C // 8):
                w[j] = w[j] + inc[8 * j - half:8 * j - half + 8, :]
        solve_half(half)
        for b in range(nb):
            u_full = jnp.concatenate([jnp.concatenate(u[b, hp], axis=0) for hp in range(H // 2)], axis=1)
            y_ref[b, sl, :] = y0_ref[b] + ar[b][C:2 * C] + _dot3(prb_ref[b], _head_stack(u_full, H))
            upd = _dot3(jnp.concatenate([u_full, v_ref[b, sl, :]], axis=0).T,
                        jnp.concatenate([bt_ref[b, sl, :], kt_ref[b, sl, :]], axis=0))
            wl = wl_ref[b, pl.ds(c0 + C - 1, 1), :]
            s_ref[b] = (s0[b] + jnp.where(bd, upd, 0.0)) * wl

    bufs_a, bufs_b = chunk_bufs[0:4], chunk_bufs[4:8]
    prep(0, bufs_a)

    def pair_body(k, carry):
        ci = 2 * k
        prep(ci + 1, bufs_b)
        seq(ci, bufs_a)
        prep(jnp.minimum(ci + 2, n_chunks - 1), bufs_a)
        seq(ci + 1, bufs_b)
        return carry

    lax.fori_loop(0, n_chunks // 2, pair_body, 0)

    for b in range(nb):
        y = y_ref[b]
        mean = _dot_01(y, bo, 2) * (1.0 / HEAD_DIM)
        d = y - mean
        var = _dot_01(d * d, bo, 2) * (1.0 / HEAD_DIM)
        yn = d * lax.rsqrt(var + RWKV_GN_EPS) * row(_RV_LNG) + row(_RV_LNB)
        bonus = _dot_01(bon_ref[b], bo, 2) * v_ref[b]
        o_ref[b] = (yn + bonus) * g_ref[b]


def _rwkv(rw, prm, l):
    B, S, _ = rw.shape
    ts = RWKV_TS
    full = lambda a: pl.BlockSpec(a.shape, lambda s: (0,) * a.ndim)
    args = [prm["rwkv_mu"][l], prm["rwkv_vec"][l], prm["rwkv_w2"][l], prm["rwkv_a2"][l], prm["rwkv_g2"][l],
            prm["rwkv_lt"], prm["block_ones"]]
    big = lambda: pltpu.VMEM((B, ts, RWKV_W), F32)
    return pl.pallas_call(
        _rwkv_kernel,
        grid=(S // ts,),
        in_specs=[pl.BlockSpec((B, ts, 1024), lambda s: (0, s, 0))] + [full(a) for a in args],
        out_specs=pl.BlockSpec((B, ts, RWKV_W), lambda s: (0, s, 0)),
        out_shape=jax.ShapeDtypeStruct((B, S, RWKV_W), F32),
        scratch_shapes=[pltpu.VMEM((B, 8, 1024), F32), pltpu.VMEM((B, RWKV_W, RWKV_W), F32)]
                       + [big() for _ in range(9)]
                       + [pltpu.VMEM((B, RWKV_C, RWKV_W), F32) for _ in range(8)],
        compiler_params=_cparams(("arbitrary",)),
        name="rwkv",
    )(rw, *args)


def _ret_kernel(p_ref, cos_ref, sin_ref, dm_ref, qd_ref, kd_ref, dmat_ref, bo_ref, gn_ref, o_ref, r_ref):
    C = p_ref.shape[1]
    W = RET_W
    H = RET_HEADS

    @pl.when(pl.program_id(1) == 0)
    def _():
        r_ref[...] = jnp.zeros_like(r_ref)

    p = p_ref[0]
    cos, sin = cos_ref[...], sin_ref[...]
    first_half = (_iota((C, W), 1) % HEAD_DIM) < (HEAD_DIM // 2)

    def rope(x):
        partner = jnp.where(first_half, pltpu.roll(x, W - HEAD_DIM // 2, 1), pltpu.roll(x, HEAD_DIM // 2, 1))
        return x * cos + partner * sin

    q = rope(p[:, 0:W])
    k = rope(p[:, W:2 * W]) * (HEAD_DIM ** -0.5)
    v = p[:, 2 * W:3 * W]
    g = p[:, 3 * W:4 * W]
    state = r_ref[...]
    scores = _dot_nt(q, _head_stack(k, H)) * dm_ref[...]
    y = _dot(scores, _head_stack(v, H)) + _dot(q * qd_ref[...], state)
    kv = _dot((k * kd_ref[...]).T, v)
    bd = (_iota((W, W), 0) // HEAD_DIM) == (_iota((W, W), 1) // HEAD_DIM)
    r_ref[...] = state * dmat_ref[...] + jnp.where(bd, kv, 0.0)

    bo = bo_ref[...]
    mean = _dot_01(y, bo, 2) * (1.0 / HEAD_DIM)
    d = y - mean
    var = _dot_01(d * d, bo, 2) * (1.0 / HEAD_DIM)
    o_ref[0] =d * lax.rsqrt(var + RET_GN_EPS) * gn_ref[...] * (g * _sigmoid(g))


def _retention(rt, prm, l):
    B, S, _ = rt.shape
    C = RET_C
    cst = lambda a: pl.BlockSpec(a.shape, lambda b, s: (0,) * a.ndim)
    consts = [prm["ret_dmask"], prm["ret_qdec"], prm["ret_kdec"], prm["ret_dmat"], prm["block_ones"],
              prm["ret_gn"][l]]
    return pl.pallas_call(
        _ret_kernel,
        grid=(B, S // C),
        in_specs=[pl.BlockSpec((1, C, 1024), lambda b, s: (b, s, 0)),
                  pl.BlockSpec((C, RET_W), lambda b, s: (s, 0)),
                  pl.BlockSpec((C, RET_W), lambda b, s: (s, 0))] + [cst(a) for a in consts],
        out_specs=pl.BlockSpec((1, C, RET_W), lambda b, s: (b, s, 0)),
        out_shape=jax.ShapeDtypeStruct((B, S, RET_W), F32),
        scratch_shapes=[pltpu.VMEM((RET_W, RET_W), F32)],
        compiler_params=_cparams(("arbitrary", "arbitrary")),
        name="retention",
    )(rt, prm["ret_cos"], prm["ret_sin"], *consts)


def _ret_tables(S):
    C, H, d = RET_C, RET_HEADS, HEAD_DIM
    half = d // 2
    inv = ROPE_BASE ** (-jnp.arange(half, dtype=F32) / half)
    ang = jnp.arange(S, dtype=F32)[:, None] * inv[None, :]
    cos, sin = jnp.cos(ang), jnp.sin(ang)
    cos_t = jnp.tile(jnp.concatenate([cos, cos], axis=1), (1, H))
    sin_t = jnp.tile(jnp.concatenate([-sin, sin], axis=1), (1, H))
    log_gamma = jnp.log1p(-jnp.exp2(-5.0 - jnp.arange(H, dtype=F32)))
    idx = jnp.arange(C, dtype=F32)
    diff = idx[:, None] - idx[None, :]
    dmask = jnp.where(diff >= 0, jnp.exp(jnp.maximum(diff, 0.0) * log_gamma[:, None, None]), 0.0)
    dmask = jnp.transpose(dmask, (1, 0, 2)).reshape(C, H * C)
    rep = lambda a: jnp.repeat(a, d, axis=1)
    kdec = rep(jnp.exp((C - 1.0 - idx)[:, None] * log_gamma[None, :]))
    qdec = rep(jnp.exp((idx + 1.0)[:, None] * log_gamma[None, :]))
    hid = np.arange(H * d) // d
    bd = jnp.asarray(hid[:, None] == hid[None, :])
    dmat = jnp.where(bd, jnp.repeat(jnp.exp(C * log_gamma), d)[:, None], 0.0)
    return {"ret_cos": cos_t, "ret_sin": sin_t, "ret_dmask": dmask, "ret_qdec": qdec, "ret_kdec": kdec,
            "ret_dmat": dmat}


def _layer_norm(z, g, b):
    mu = jnp.mean(z, axis=1, keepdims=True)
    d = z - mu
    var = jnp.mean(d * d, axis=1, keepdims=True)
    return d * lax.rsqrt(var + LN_EPS) * g + b


ROW_TILES = D_MODEL // LANES


def _out_kernel(yf_ref, yr_ref, yt_ref, x_ref, w_ref, g_ref, b_ref, wr_ref, br_ref,
                x1_ref, info_ref, info_t_ref):
    tm = x_ref.shape[0]
    mixed = (_dot(yf_ref[...], w_ref[0:FOX_W, :])
             + _dot(yr_ref[...].astype(BF16), w_ref[FOX_W:FOX_W + RWKV_W, :])
             + _dot(yt_ref[...].astype(BF16), w_ref[FOX_W + RWKV_W:, :]))
    x1 = _layer_norm(ALPHA * x_ref[...] + mixed, g_ref[...], b_ref[...])
    x1_ref[...] = x1

    logits = _dot3(x1, wr_ref[...]) + br_ref[...]
    lane = _iota((tm, LANES), 1).astype(F32)
    ninf = -jnp.inf
    first = lambda hit: jnp.min(jnp.where(hit, lane, float(LANES)), axis=1, keepdims=True)
    gl = jnp.where(lane < N_GROUPS, logits, ninf)
    gmax = jnp.max(gl, axis=1, keepdims=True)
    grp = first(gl == gmax)
    g_gate = 1.0 / jnp.sum(jnp.exp(gl - gmax), axis=1, keepdims=True)
    lo = N_GROUPS + grp * EXPERTS_PER_GROUP
    el = jnp.where(jnp.logical_and(lane >= lo, lane < lo + EXPERTS_PER_GROUP), logits, ninf)
    v1 = jnp.max(el, axis=1, keepdims=True)
    i1 = first(el == v1)
    el2 = jnp.where(lane == i1, ninf, el)
    v2 = jnp.max(el2, axis=1, keepdims=True)
    i2 = first(el2 == v2)
    e = jnp.exp(v2 - v1)
    den = 1.0 / (1.0 + e)
    info = jnp.where(lane == 0, i1 - N_GROUPS, jnp.where(lane == 1, i2 - N_GROUPS,
           jnp.where(lane == 2, g_gate * den, jnp.where(lane == 3, g_gate * e * den, 0.0))))
    info_ref[...] = info
    info_t_ref[...] = info.T[0:8, :]


def _out_proj(yf, yr, yt, x, prm, l):
    T = x.shape[0]
    tm = OUT_TM
    rows = lambda w: pl.BlockSpec((tm, w), lambda i: (i, 0))
    cst = lambda a: pl.BlockSpec(a.shape, lambda i: (0,) * a.ndim)
    consts = [prm["w_out"][l], prm["ln1_g"][l], prm["ln1_b"][l], prm["w_router"][l], prm["b_router"][l]]
    return pl.pallas_call(
        _out_kernel,
        grid=(T // tm,),
        in_specs=[rows(FOX_W), rows(RWKV_W), rows(RET_W), rows(D_MODEL)] + [cst(a) for a in consts],
        out_specs=[rows(D_MODEL), rows(LANES), pl.BlockSpec((8, tm), lambda i: (0, i))],
        out_shape=[jax.ShapeDtypeStruct((T, D_MODEL), F32), jax.ShapeDtypeStruct((T, LANES), F32),
                   jax.ShapeDtypeStruct((8, T), F32)],
        compiler_params=_cparams(("arbitrary",)),
        name="out_proj",
    )(yf, yr, yt, x, *consts)


def _rank_kernel(it_ref, tri_ref, rk_ref, cnt_ref, carry_ref):
    tm = it_ref.shape[1]

    @pl.when(pl.program_id(0) == 0)
    def _():
        carry_ref[...] = jnp.zeros_like(carry_ref)

    sub = _iota((N_EXPERTS, tm), 0).astype(F32)
    oh1 = sub == it_ref[0:1, :]
    oh2 = sub == it_ref[1:2, :]
    oh = jnp.where(jnp.logical_or(oh1, oh2), 1.0, 0.0)
    before = _dot(oh.astype(BF16), tri_ref[...]) + carry_ref[:, 0:1]
    r1 = jnp.sum(jnp.where(oh1, before, 0.0), axis=0, keepdims=True)
    r2 = jnp.sum(jnp.where(oh2, before, 0.0), axis=0, keepdims=True)
    rk_ref[...] = jnp.concatenate([r1, r2, jnp.zeros((6, tm), F32)], axis=0)
    carry_ref[...] = carry_ref[...] + jnp.sum(oh, axis=1, keepdims=True)
    cnt_ref[...] = carry_ref[...]


def _rank(info_t, tri):
    T = info_t.shape[1]
    tm = RANK_TM
    return pl.pallas_call(
        _rank_kernel,
        grid=(T // tm,),
        in_specs=[pl.BlockSpec((8, tm), lambda i: (0, i)), pl.BlockSpec((tm, tm), lambda i: (0, 0))],
        out_specs=[pl.BlockSpec((8, tm), lambda i: (0, i)), pl.BlockSpec((N_EXPERTS, LANES), lambda i: (0, 0))],
        out_shape=[jax.ShapeDtypeStruct((8, T), F32), jax.ShapeDtypeStruct((N_EXPERTS, LANES), F32)],
        scratch_shapes=[pltpu.VMEM((N_EXPERTS, LANES), F32)],
        compiler_params=_cparams(("arbitrary",)),
        name="rank",
    )(info_t, tri)


def _row_copy(src_hbm, row, dst, r, sem):
    return pltpu.make_async_copy(src_hbm.at[row], dst.at[:, r, :], sem)


def _expert_kernel(be_ref, nu_ref, src0_ref, src1_ref, x_hbm, w1_ref, w3_ref, w2_ref, y_ref,
                   xbuf, sem, w1b, w3b, w2b):
    j = pl.program_id(0)
    tm = xbuf.shape[2]
    nu = nu_ref[0]
    slot = j % 2

    def gather(src_ref, s):
        for r in range(tm):
            _row_copy(x_hbm, src_ref[0, 0, r], xbuf.at[s], r, sem.at[s]).start(priority=r % 2)

    def drain(s):
        def body(r, c):
            _row_copy(x_hbm, 0, xbuf.at[s], r, sem.at[s]).wait()
            return c
        lax.fori_loop(0, tm, body, 0, unroll=8)

    @pl.when(j == 0)
    def _():
        gather(src0_ref, 0)

    @pl.when(j < nu)
    def _():
        drain(slot)
        gather(src1_ref, 1 - slot)

    @pl.when(jnp.logical_and(j < nu, jnp.logical_or(j == 0, be_ref[j] != be_ref[jnp.maximum(j - 1, 0)])))
    def _():
        w1b[...] = w1_ref[0].astype(BF16)
        w3b[...] = w3_ref[0].astype(BF16)
        w2b[...] = w2_ref[0].astype(BF16)

    @pl.when(j < nu)
    def _():
        x = jnp.concatenate([xbuf[slot, c] for c in range(ROW_TILES)], axis=1).astype(BF16)
        a = _dot(x, w1b[...])
        h = (a * _sigmoid(a)) * _dot(x, w3b[...])
        y = _dot(h.astype(BF16), w2b[...])
        for c in range(ROW_TILES):
            y_ref[:, c, :] = y[:, c * LANES:(c + 1) * LANES]

    @pl.when(j == nu)
    def _():
        drain(slot)

    @pl.when(j >= nu)
    def _():
        y_ref[...] = jnp.zeros_like(y_ref)


def _experts(block_expert, n_used, src_tok, x1, w1, w3, w2):
    n_blk = block_expert.shape[0]
    tm = MOE_TM
    src_tok = src_tok.reshape(n_blk, 1, tm)
    grid_spec = pltpu.PrefetchScalarGridSpec(
        num_scalar_prefetch=2,
        grid=(n_blk,),
        in_specs=[pl.BlockSpec((1, 1, tm), lambda j, be, nu: (0, 0, 0), memory_space=pltpu.SMEM),
                  pl.BlockSpec((1, 1, tm), lambda j, be, nu: (jnp.minimum(j + 1, n_blk - 1), 0, 0),
                               memory_space=pltpu.SMEM),
                  pl.BlockSpec(memory_space=pl.ANY),
                  pl.BlockSpec((1, D_MODEL, D_EXPERT), lambda j, be, nu: (be[j], 0, 0)),
                  pl.BlockSpec((1, D_MODEL, D_EXPERT), lambda j, be, nu: (be[j], 0, 0)),
                  pl.BlockSpec((1, D_EXPERT, D_MODEL), lambda j, be, nu: (be[j], 0, 0))],
        out_specs=pl.BlockSpec((tm, ROW_TILES, LANES), lambda j, be, nu: (j, 0, 0)),
        scratch_shapes=[pltpu.VMEM((2, ROW_TILES, tm, LANES), F32), pltpu.SemaphoreType.DMA((2,)),
                        pltpu.VMEM((D_MODEL, D_EXPERT), BF16), pltpu.VMEM((D_MODEL, D_EXPERT), BF16),
                        pltpu.VMEM((D_EXPERT, D_MODEL), BF16)],
    )
    return pl.pallas_call(
        _expert_kernel,
        grid_spec=grid_spec,
        out_shape=jax.ShapeDtypeStruct((n_blk * tm, ROW_TILES, LANES), F32),
        compiler_params=_cparams(("arbitrary",)),
        name="experts",
    )(block_expert, n_used, src_tok, src_tok, x1, w1, w3, w2)


def _combine_kernel(dest0_ref, dest1_ref, y_hbm, info_ref, x_ref, g_ref, b_ref, o_ref, ybuf, sem):
    i = pl.program_id(0)
    tm = x_ref.shape[0]
    slot = i % 2

    def gather(dest_ref, s):
        for r in range(2 * tm):
            _row_copy(y_hbm, dest_ref[0, 0, r], ybuf.at[s], r, sem.at[s]).start(priority=r % 2)

    @pl.when(i == 0)
    def _():
        gather(dest0_ref, 0)

    @pl.when(i + 1 < pl.num_programs(0))
    def _():
        gather(dest1_ref, 1 - slot)

    def drain(r, c):
        _row_copy(y_hbm, 0, ybuf.at[slot], r, sem.at[slot]).wait()
        return c
    lax.fori_loop(0, 2 * tm, drain, 0, unroll=8)
    info = info_ref[...]
    rows = lambda lo: jnp.concatenate([ybuf[slot, c, pl.ds(lo, tm), :] for c in range(ROW_TILES)], axis=1)
    moe = info[:, 2:3] * rows(0) + info[:, 3:4] * rows(tm)
    o_ref[...] = _layer_norm(ALPHA * x_ref[...] + moe, g_ref[...], b_ref[...])


def _combine(dest, y_pad, info, x1, g, b):
    T = x1.shape[0]
    tm = COMB_TM
    cst = lambda a: pl.BlockSpec(a.shape, lambda i: (0,) * a.ndim)
    n = T // tm
    return pl.pallas_call(
        _combine_kernel,
        grid=(n,),
        in_specs=[pl.BlockSpec((1, 1, 2 * tm), lambda i: (0, 0, 0), memory_space=pltpu.SMEM),
                  pl.BlockSpec((1, 1, 2 * tm), lambda i: (jnp.minimum(i + 1, n - 1), 0, 0),
                               memory_space=pltpu.SMEM),
                  pl.BlockSpec(memory_space=pl.ANY),
                  pl.BlockSpec((tm, LANES), lambda i: (i, 0)),
                  pl.BlockSpec((tm, D_MODEL), lambda i: (i, 0)), cst(g), cst(b)],
        out_specs=pl.BlockSpec((tm, D_MODEL), lambda i: (i, 0)),
        out_shape=jax.ShapeDtypeStruct((T, D_MODEL), F32),
        scratch_shapes=[pltpu.VMEM((2, ROW_TILES, 2 * tm, LANES), F32), pltpu.SemaphoreType.DMA((2,))],
        compiler_params=_cparams(("arbitrary",)),
        name="combine",
    )(dest, dest, y_pad, info, x1, g, b)


def _moe(x1, x1t, info, info_t, prm, l):
    T = x1.shape[0]
    tm = MOE_TM
    n_blk = (2 * T) // tm + N_EXPERTS
    ranks, cnt = _rank(info_t, prm["tri_rank"])
    counts = cnt[:, 0].astype(jnp.int32)
    padded = ((counts + tm - 1) // tm) * tm
    pad_end = jnp.cumsum(padded)
    pad_off = pad_end - padded
    eid = info_t[0:2].astype(jnp.int32)
    e_iota = jnp.arange(N_EXPERTS, dtype=jnp.int32)
    off = jnp.sum(jnp.where(eid[:, None, :] == e_iota[None, :, None], pad_off[None, :, None], 0), axis=1)
    dest = off + ranks[0:2].astype(jnp.int32)
    tok = jnp.broadcast_to(jnp.arange(T, dtype=jnp.int32), (2, T))
    src_tok = jnp.zeros((n_blk * tm,), jnp.int32).at[dest.reshape(-1)].set(tok.reshape(-1))
    blk_start = jnp.arange(n_blk, dtype=jnp.int32) * tm
    block_expert = jnp.minimum(jnp.sum((pad_end[None, :] <= blk_start[:, None]).astype(jnp.int32), axis=1),
                               N_EXPERTS - 1)
    n_used = (pad_end[-1] // tm).astype(jnp.int32).reshape(1)
    block_expert = jnp.where(jnp.arange(n_blk) < n_used[0], block_expert,
                             block_expert[jnp.maximum(n_used[0] - 1, 0)])
    flat = lambda w: w.reshape((-1,) + w.shape[2:])
    y_pad = _experts(block_expert + l * N_EXPERTS, n_used, src_tok, x1t,
                     flat(prm["moe_w1"]), flat(prm["moe_w3"]), flat(prm["moe_w2"]))
    ct = COMB_TM
    dest_blk = dest.reshape(2, T // ct, ct).transpose(1, 0, 2).reshape(T // ct, 1, 2 * ct)
    return _combine(dest_blk, y_pad, info, x1, prm["ln2_g"][l], prm["ln2_b"][l])


def _prep_params(p):
    L = p["w_in"].shape[0]
    w = p["w_in"]
    fg = jnp.pad(w[:, :, 1536:1544], ((0, 0), (0, 0), (0, LANES - FOX_HEADS)))
    w_in = jnp.concatenate([w[:, :, 0:1536], fg, w[:, :, 1544:2568], w[:, :, 2568:3592]], axis=-1)
    out = {
        "w_in": w_in.astype(BF16),
        "fbias": jnp.pad(p["fox_fgate_bias"], ((0, 0), (0, LANES - FOX_HEADS)))[:, None, :],
        "tri_proj": jnp.tril(jnp.ones((PROJ_TM, PROJ_TM), BF16)),
    }
    vec = jnp.stack([p["rwkv_w0"], p["rwkv_a0"], p["rwkv_k_k"], p["rwkv_k_a"],
                     p["rwkv_r_k"].reshape(L, RWKV_W), p["rwkv_ln_g"], p["rwkv_ln_b"],
                     jnp.zeros((L, RWKV_W), F32)], axis=1)
    ti = np.arange(RWKV_TS)
    lt = ((ti[:, None] // RWKV_C == ti[None, :] // RWKV_C) & (ti[None, :] <= ti[:, None])).astype(np.float32)
    hi = np.arange(RWKV_W) // HEAD_DIM
    out.update({
        "rwkv_mu": jnp.concatenate([p["rwkv_mu_rkv"].reshape(L, 3 * RWKV_W), p["rwkv_mu_lora"]], axis=-1)[:, None, :],
        "rwkv_vec": vec,
        "rwkv_w2": p["rwkv_w2"], "rwkv_a2": p["rwkv_a2"], "rwkv_g2": p["rwkv_g2"],
        "rwkv_lt": jnp.asarray(lt, dtype=BF16),
        "block_ones": jnp.asarray((hi[:, None] == hi[None, :]).astype(np.float32), dtype=BF16),
    })
    out.update(_ret_tables(p["x"].shape[1]))
    out["ret_gn"] = p["ret_gn_g"][:, None, :]
    pad_r = LANES - N_GROUPS - N_EXPERTS
    ri = np.arange(RANK_TM)
    out.update({
        "w_out": p["w_out"].astype(BF16),
        "ln1_g": p["ln1_g"][:, None, :], "ln1_b": p["ln1_b"][:, None, :],
        "ln2_g": p["ln2_g"][:, None, :], "ln2_b": p["ln2_b"][:, None, :],
        "w_router": jnp.pad(jnp.concatenate([p["moe_w_group"], p["moe_w_expert"]], axis=-1),
                            ((0, 0), (0, 0), (0, pad_r))),
        "b_router": jnp.pad(jnp.concatenate([p["moe_b_group"], p["moe_b_expert"]], axis=-1),
                            ((0, 0), (0, pad_r)))[:, None, :],
        "tri_rank": jnp.asarray((ri[:, None] < ri[None, :]).astype(np.float32), dtype=BF16),
        "moe_w1": p["moe_w1"], "moe_w3": p["moe_w3"], "moe_w2": p["moe_w2"],
    })
    return out


def kernel(x, w_in, fox_fgate_bias, rwkv_mu_rkv, rwkv_mu_lora, rwkv_w0, rwkv_w2, rwkv_a0, rwkv_a2, rwkv_g2,
           rwkv_k_k, rwkv_k_a, rwkv_r_k, rwkv_ln_g, rwkv_ln_b, ret_gn_g, w_out, ln1_g, ln1_b, ln2_g, ln2_b,
           moe_w_group, moe_b_group, moe_w_expert, moe_b_expert, moe_w1, moe_w3, moe_w2):
    prm = _prep_params(dict(
        x=x, w_in=w_in, fox_fgate_bias=fox_fgate_bias, rwkv_mu_rkv=rwkv_mu_rkv, rwkv_mu_lora=rwkv_mu_lora,
        rwkv_w0=rwkv_w0, rwkv_w2=rwkv_w2, rwkv_a0=rwkv_a0, rwkv_a2=rwkv_a2, rwkv_g2=rwkv_g2,
        rwkv_k_k=rwkv_k_k, rwkv_k_a=rwkv_k_a, rwkv_r_k=rwkv_r_k, rwkv_ln_g=rwkv_ln_g, rwkv_ln_b=rwkv_ln_b,
        ret_gn_g=ret_gn_g, w_out=w_out, ln1_g=ln1_g, ln1_b=ln1_b, ln2_g=ln2_g, ln2_b=ln2_b,
        moe_w_group=moe_w_group, moe_b_group=moe_b_group, moe_w_expert=moe_w_expert,
        moe_b_expert=moe_b_expert, moe_w1=moe_w1, moe_w3=moe_w3, moe_w2=moe_w2))
    B, S, D = x.shape
    T = B * S
    for l in range(w_in.shape[0]):
        qa, ka, va, rw, rt, stats = _project(x, prm["w_in"][l], prm["fbias"][l], prm["tri_proj"])
        y_fox = _fox_attention(qa, ka, va, stats)
        y_rwkv = _rwkv(rw, prm, l)
        y_ret = _retention(rt, prm, l)
        x1, info, info_t = _out_proj(y_fox.reshape(T, FOX_W), y_rwkv.reshape(T, RWKV_W),
                                     y_ret.reshape(T, RET_W), x.reshape(T, D), prm, l)
        x = _moe(x1, x1.reshape(T, ROW_TILES, LANES), info, info_t, prm, l).reshape(B, S, D)
    return x
```

```python
import functools
import math

import jax
import jax.numpy as jnp
import numpy as np
from jax import lax
from jax.experimental import pallas as pl
from jax.experimental.pallas import tpu as pltpu

F32 = jnp.float32
BF16 = jnp.bfloat16
HIGHEST = lax.Precision.HIGHEST

D_MODEL = 1024
DEPTH = 4
HEAD_DIM = 64
FOX_W, RWKV_W, RET_W = 512, 256, 256
FOX_HEADS, RWKV_HEADS, RET_HEADS = 8, 4, 4
W_LORA, A_LORA, G_LORA = 64, 64, 128
ROPE_BASE = 10000.0
N_GROUPS, EXPERTS_PER_GROUP, N_EXPERTS, D_EXPERT = 4, 8, 32, 512
ALPHA = (2.0 * DEPTH) ** 0.25
LN_EPS = 1e-5
LOG2E = math.log2(math.e)
RWKV_GN_EPS = 64e-5
RET_GN_EPS = 1e-6

LANES = 128
VMEM_LIMIT = 56 * 1024 * 1024

PROJ_TM = 512
ATT_T = 1024
ATT_HEADS = 4
ATT_SKIP_MARGIN = 160.0
RWKV_TS = 512
RWKV_C = 64
RET_C = 256
OUT_TM = 512
RANK_TM = 512
MOE_TM = 256
COMB_TM = 256

P_FOX = 0
P_FG = 1536
P_RWKV = 1664
P_RET = 2688
P_TOT = 3712


def _cparams(sem):
    return pltpu.CompilerParams(dimension_semantics=sem, vmem_limit_bytes=VMEM_LIMIT)


def _dot(a, b, precision=None):
    return jnp.dot(a, b, preferred_element_type=F32, precision=precision)


def _dot_nt(a, b, precision=None):
    return lax.dot_general(a, b, (((1,), (1,)), ((), ())), preferred_element_type=F32,
                           precision=precision)


def _split(a, n):
    parts, r = [], a
    for i in range(n):
        h = r.astype(BF16)
        parts.append(h)
        if i + 1 < n:
            r = r - h.astype(F32)
    return parts


def _dot3(a, b, nt=False):
    d = _dot_nt if nt else _dot
    a1, a2 = _split(a, 2)
    b1, b2 = _split(b, 2)
    return (d(a1, b2) + d(a2, b1)) + d(a1, b1)


def _dot_01(a, m, n=3, left=False):
    d = (lambda p: _dot(m, p)) if left else (lambda p: _dot(p, m))
    parts = _split(a, n)
    acc = d(parts[-1])
    for p in parts[-2::-1]:
        acc = acc + d(p)
    return acc


def _log_sigmoid(z):
    return -(jnp.maximum(-z, 0.0) + jnp.log(1.0 + jnp.exp(-jnp.abs(z))))


def _sigmoid(z):
    return 1.0 / (1.0 + jnp.exp(-z))


def _iota(shape, dim):
    return lax.broadcasted_iota(jnp.int32, shape, dim)


def _proj_kernel(x_ref, w_ref, fb_ref, tri_ref, qa_ref, ka_ref, va_ref, rw_ref, rt_ref, stat_ref, carry_ref):
    tm = x_ref.shape[1]

    @pl.when(pl.program_id(1) == 0)
    def _():
        carry_ref[...] = jnp.zeros_like(carry_ref)

    xb = x_ref[0].astype(BF16)
    p_fox = _dot(xb, w_ref[:, P_FOX:P_FG])
    z = _dot(xb, w_ref[:, P_FG:P_RWKV]) + fb_ref[...]
    rw_ref[0] = _dot(xb, w_ref[:, P_RWKV:P_RET])
    rt_ref[0] = _dot(xb, w_ref[:, P_RET:P_TOT])

    c = _dot_01(_log_sigmoid(z), tri_ref[...], 3, left=True) + carry_ref[0:1, :]
    carry_ref[...] = jnp.broadcast_to(c[tm - 1:tm, :], carry_ref.shape)

    lane = _iota((tm, LANES), 1)
    for h in range(FOX_HEADS):
        cb = jnp.broadcast_to(c[:, h:h + 1], (tm, LANES)) * LOG2E
        hi = cb.astype(BF16).astype(F32)
        r1 = cb - hi
        mid = r1.astype(BF16).astype(F32)
        lo = r1 - mid
        tile = (h // 2) * LANES

        def head_lanes(base):
            t = p_fox[:, base + tile: base + tile + LANES]
            return pltpu.roll(t, 64, 1) if h % 2 else t

        q = head_lanes(0) * (LOG2E * HEAD_DIM ** -0.5)
        k = head_lanes(FOX_W)
        v = head_lanes(2 * FOX_W)
        qa = jnp.where(lane < 64, q, jnp.where(lane == 64, hi, jnp.where(lane == 65, mid,
             jnp.where(lane == 66, lo, jnp.where(lane < 70, 1.0, 0.0)))))
        ka = jnp.where(lane < 64, k, jnp.where(lane < 67, 1.0, jnp.where(lane == 67, -hi,
             jnp.where(lane == 68, -mid, jnp.where(lane == 69, -lo, 0.0)))))
        va = jnp.where(lane < 64, v, jnp.where(lane == 64, 1.0, 0.0))
        qa, ka = qa.astype(BF16), ka.astype(BF16)
        qa_ref[0, h] = qa
        ka_ref[0, h] = ka
        va_ref[0, h] = va.astype(BF16)
        qr = jnp.where(lane < 64, qa.astype(F32), 0.0)
        kr = jnp.where(lane < 64, ka.astype(F32), 0.0)
        col_max = lambda a: jnp.max(jnp.sum(a, axis=1, keepdims=True), axis=0, keepdims=True)
        l1 = _iota((1, LANES), 1)
        stat_ref[0, 0, h:h + 1, :] = jnp.where(
            l1 == 0, col_max(qr * qr), jnp.where(l1 == 1, col_max(kr * kr), jnp.where(
                l1 == 2, cb[0:1, :], jnp.where(l1 == 3, cb[tm - 1:tm, :], 0.0))))


def _project(x, w_bf16, fbias, tri):
    B, S, D = x.shape
    tm = PROJ_TM
    aug = jax.ShapeDtypeStruct((B, FOX_HEADS, S, LANES), BF16)
    wide = jax.ShapeDtypeStruct((B, S, 1024), F32)
    aug_spec = pl.BlockSpec((1, FOX_HEADS, tm, LANES), lambda b, s: (b, 0, s, 0))
    wide_spec = pl.BlockSpec((1, tm, 1024), lambda b, s: (b, s, 0))
    return pl.pallas_call(
        _proj_kernel,
        grid=(B, S // tm),
        in_specs=[pl.BlockSpec((1, tm, D), lambda b, s: (b, s, 0)),
                  pl.BlockSpec((D, P_TOT), lambda b, s: (0, 0)),
                  pl.BlockSpec((1, LANES), lambda b, s: (0, 0)),
                  pl.BlockSpec((tm, tm), lambda b, s: (0, 0))],
        out_specs=[aug_spec, aug_spec, aug_spec, wide_spec, wide_spec,
                   pl.BlockSpec((1, 1, FOX_HEADS, LANES), lambda b, s: (b, s, 0, 0))],
        out_shape=[aug, aug, aug, wide, wide, jax.ShapeDtypeStruct((B, S // tm, FOX_HEADS, LANES), F32)],
        scratch_shapes=[pltpu.VMEM((8, LANES), F32)],
        compiler_params=_cparams(("arbitrary", "arbitrary")),
        name="proj",
    )(x, w_bf16, fbias, tri)


def _fox_kernel(qi_ref, ki_ref, skip_ref, kv_ref, q_ref, k_ref, v_ref, o_ref, m_ref, acc_ref):
    p = pl.program_id(2)
    qi = qi_ref[p]
    ki = ki_ref[p]
    nh, t = q_ref.shape[1], q_ref.shape[2]
    skip = skip_ref[(pl.program_id(0) * pl.num_programs(1) + pl.program_id(1)) * pl.num_programs(2) + p]

    @pl.when(ki == qi)
    def _():
        m_ref[...] = jnp.full_like(m_ref, -jnp.inf)
        acc_ref[...] = jnp.zeros_like(acc_ref)

    def step(masked):
        qk = lambda j: _dot_nt(q_ref[0, j], k_ref[0, j])
        ss = {j: qk(j) for j in range(min(2, nh))}
        for j in range(nh):
            if j + 2 < nh:
                ss[j + 2] = qk(j + 2)
            s = ss.pop(j)
            if masked:
                s = jnp.where(_iota((t, t), 1) <= _iota((t, t), 0), s, -jnp.inf)
            m_prev = m_ref[j]
            m_new = jnp.maximum(m_prev, jnp.max(s, axis=1, keepdims=True))
            alpha = jnp.exp2(m_prev - m_new)
            pm = jnp.exp2(s - jnp.concatenate([m_new] * (t // LANES), axis=1))
            acc_ref[j] = alpha * acc_ref[j] + _dot(pm.astype(BF16), v_ref[0, j])
            m_ref[j] = m_new

    @pl.when(ki == qi)
    def _():
        step(True)

    @pl.when(jnp.logical_and(ki < qi, skip == 0))
    def _():
        step(False)

    @pl.when(ki == 0)
    def _():
        lane = _iota((t, LANES), 1)
        for jp in range(nh // 2):
            a0, a1 = acc_ref[2 * jp], acc_ref[2 * jp + 1]
            o0, o1 = a0 / a0[:, 64:65], a1 / a1[:, 64:65]
            o_ref[0, :, jp * LANES:(jp + 1) * LANES] = jnp.where(
                lane < 64, o0, pltpu.roll(o1, 64, 1)).astype(o_ref.dtype)


def _fox_skip_flags(stats, qi_arr, ki_arr):
    B, n_sub, H, _ = stats.shape
    r = ATT_T // PROJ_TM
    st = stats.reshape(B, n_sub // r, r, H, LANES)
    qn = jnp.sqrt(jnp.max(st[..., 0], axis=2))
    kn = jnp.sqrt(jnp.max(st[..., 1], axis=2))
    c_first = st[:, :, 0, :, 2]
    c_last = st[:, :, r - 1, :, 3]
    qi, ki = jnp.asarray(qi_arr), jnp.asarray(ki_arr)
    bound = qn[:, qi] * kn[:, ki] + c_first[:, qi] - c_last[:, ki]
    dead = bound < -(qn * kn)[:, qi] - ATT_SKIP_MARGIN
    dead = jnp.all(dead.reshape(B, -1, H // ATT_HEADS, ATT_HEADS), axis=-1)
    return jnp.transpose(dead, (0, 2, 1)).astype(jnp.int32).reshape(-1)


def _fox_attention(qa, ka, va, stats):
    B, H, S, _ = qa.shape
    t = ATT_T
    nh = ATT_HEADS
    nq = S // t
    qi_arr = np.concatenate([np.full(i + 1, i, np.int32) for i in range(nq)])
    ki_arr = np.concatenate([np.arange(i, -1, -1, dtype=np.int32) for i in range(nq)])
    n_pairs = qi_arr.shape[0]
    skip = _fox_skip_flags(stats, qi_arr, ki_arr)
    n_groups = H // nh
    pos = jnp.where(skip.reshape(-1, n_pairs) == 0, jnp.arange(n_pairs, dtype=jnp.int32), -1)
    kv_blk = jnp.asarray(ki_arr)[lax.cummax(pos, axis=1)].reshape(-1)
    flat = lambda b, hp, p: (b * n_groups + hp) * n_pairs + p
    grid_spec = pltpu.PrefetchScalarGridSpec(
        num_scalar_prefetch=4,
        grid=(B, n_groups, n_pairs),
        in_specs=[pl.BlockSpec((1, nh, t, LANES), lambda b, hp, p, qi, ki, sk, kv: (b, hp, qi[p], 0)),
                  pl.BlockSpec((1, nh, t, LANES), lambda b, hp, p, qi, ki, sk, kv: (b, hp, kv[flat(b, hp, p)], 0)),
                  pl.BlockSpec((1, nh, t, LANES), lambda b, hp, p, qi, ki, sk, kv: (b, hp, kv[flat(b, hp, p)], 0))],
        out_specs=pl.BlockSpec((1, t, nh * HEAD_DIM), lambda b, hp, p, qi, ki, sk, kv: (b, qi[p], hp)),
        scratch_shapes=[pltpu.VMEM((nh, t, LANES), F32), pltpu.VMEM((nh, t, LANES), F32)],
    )
    return pl.pallas_call(
        _fox_kernel,
        grid_spec=grid_spec,
        out_shape=jax.ShapeDtypeStruct((B, S, FOX_W), BF16),
        compiler_params=_cparams(("arbitrary", "arbitrary", "arbitrary")),
        name="fox_attn",
    )(jnp.asarray(qi_arr), jnp.asarray(ki_arr), skip, kv_blk, qa, ka, va)


_RV_W0, _RV_A0, _RV_KK, _RV_KA, _RV_RK, _RV_LNG, _RV_LNB = range(7)


def _head_stack(x, n_heads):
    lane = _iota(x.shape, 1)
    return jnp.concatenate(
        [jnp.where(lane // HEAD_DIM == h, x, 0.0) for h in range(n_heads)], axis=0)


def _rwkv_kernel(p_ref, mu_ref, vec_ref, w2_ref, a2_ref, g2_ref, lt_ref, bo_ref, o_ref,
                 prev_ref, s_ref, at_ref, rt_ref, bt_ref, kt_ref, v_ref, wl_ref, y_ref, bon_ref, g_ref,
                 *chunk_bufs):
    nb, ts, _ = p_ref.shape
    C = RWKV_C
    H = RWKV_HEADS
    W = RWKV_W

    @pl.when(pl.program_id(0) == 0)
    def _():
        prev_ref[...] = jnp.zeros_like(prev_ref)
        s_ref[...] = jnp.zeros_like(s_ref)

    vec = vec_ref[...]
    row = lambda i: vec[i:i + 1, :]
    bo = bo_ref[...]

    for b in range(nb):
        p = p_ref[b]
        rowi = _iota(p.shape, 0)
        prev = jnp.where(rowi == 0, jnp.broadcast_to(prev_ref[b, 0:1, :], p.shape), pltpu.roll(p, 1, 0))
        prev_ref[b] = jnp.broadcast_to(p[ts - 1:ts, :], prev_ref.shape[1:])
        xm = p + (prev - p) * mu_ref[...]
        r, k, v = xm[:, 0:W], xm[:, W:2 * W], xm[:, 2 * W:3 * W]
        wd = xm[:, 3 * W:3 * W + W_LORA]
        ad = xm[:, 3 * W + W_LORA:3 * W + W_LORA + A_LORA]
        gd = xm[:, 3 * W + W_LORA + A_LORA:]
        lw = row(_RV_W0) + _dot3(jnp.tanh(wd), w2_ref[...])
        logdec = -jnp.exp(_log_sigmoid(lw) - 0.5)
        a = _sigmoid(row(_RV_A0) + _dot3(ad, a2_ref[...]))
        g_ref[b] = _dot3(_sigmoid(gd), g2_ref[...])
        kk = k * row(_RV_KK)
        kk = kk / jnp.maximum(jnp.sqrt(_dot_01(kk * kk, bo, 2)), 1e-12)
        k2 = k * (1.0 + (a - 1.0) * row(_RV_KA))
        cum = _dot_01(logdec, lt_ref[...], 3, left=True)
        wincl = jnp.exp(cum)
        winv = jnp.exp(-cum)
        at_ref[b] = -kk * jnp.exp(cum - logdec)
        rt_ref[b] = r * wincl
        bt_ref[b] = kk * a * winv
        kt_ref[b] = k2 * winv
        v_ref[b] = v
        wl_ref[b] = wincl
        bon_ref[b] = r * k2 * row(_RV_RK)

    li = _iota((C, H * C), 1) % C
    ri = _iota((C, H * C), 0)
    strict = li < ri
    incl = li <= ri
    bd = (_iota((W, W), 0) // HEAD_DIM) == (_iota((W, W), 1) // HEAD_DIM)
    n_chunks = ts // C

    def prep(ci, bufs):
        pab_ref, prb_ref, base_ref, y0_ref = bufs
        sl = pl.ds(pl.multiple_of(ci * C, C), C)
        for b in range(nb):
            at, rt, bt, kt, vv = at_ref[b, sl, :], rt_ref[b, sl, :], bt_ref[b, sl, :], kt_ref[b, sl, :], v_ref[b, sl, :]
            bstack, kstack, vstack = _head_stack(bt, H), _head_stack(kt, H), _head_stack(vv, H)
            pab_ref[b] = jnp.where(strict, _dot3(at, bstack, nt=True), 0.0)
            pak = jnp.where(strict, _dot3(at, kstack, nt=True), 0.0)
            prb_ref[b] = jnp.where(incl, _dot3(rt, bstack, nt=True), 0.0)
            prk = jnp.where(incl, _dot3(rt, kstack, nt=True), 0.0)
            base_ref[b] = _dot3(pak, vstack)
            y0_ref[b] = _dot3(prk, vstack)

    def seq(ci, bufs):
        pab_ref, prb_ref, base_ref, y0_ref = bufs
        c0 = pl.multiple_of(ci * C, C)
        sl = pl.ds(c0, C)
        s0 = [s_ref[b] for b in range(nb)]
        ar = [_dot3(jnp.concatenate([at_ref[b, sl, :], rt_ref[b, sl, :]], axis=0), s0[b], nt=True)
              for b in range(nb)]
        half = C // 2
        lo_lanes = _iota((8, LANES), 1) < HEAD_DIM
        lo_half = _iota((half, LANES), 1) < HEAD_DIM
        u = {}
        for b in range(nb):
            base = base_ref[b] + ar[b][0:C]
            for hp in range(H // 2):
                u[b, hp] = [base[8 * j:8 * j + 8, hp * LANES:(hp + 1) * LANES] for j in range(C // 8)]
        pab = {(b, hp): pab_ref[b, :, hp * LANES:(hp + 1) * LANES] for b in range(nb) for hp in range(H // 2)}

        def solve_half(t0):
            for t in range(t0, t0 + half - 1):
                for key, w in u.items():
                    u_t = jnp.broadcast_to(w[t // 8][t % 8:t % 8 + 1, :], (8, LANES))
                    for j in range((t + 1) // 8, (t0 + half) // 8):
                        rows = pab[key][8 * j:8 * j + 8, :]
                        coef = jnp.where(lo_lanes, jnp.broadcast_to(rows[:, t:t + 1], (8, LANES)),
                                         jnp.broadcast_to(rows[:, HEAD_DIM + t:HEAD_DIM + t + 1], (8, LANES)))
                        w[j] = w[j] + coef * u_t

        solve_half(0)
        for key, w in u.items():
            first = jnp.concatenate(w[0:half // 8], axis=0)
            a_lo = pab[key][half:C, 0:half]
            a_hi = pab[key][half:C, HEAD_DIM:HEAD_DIM + half]
            inc = jnp.where(lo_half, _dot3(a_lo, first), _dot3(a_hi, first))
            for j in range(half // 8, C // 8):
                w[j] = w[j] + inc[8 * j - half:8 * j - half + 8, :]
        solve_half(half)
        for b in range(nb):
            u_full = jnp.concatenate([jnp.concatenate(u[b, hp], axis=0) for hp in range(H // 2)], axis=1)
            y_ref[b, sl, :] = y0_ref[b] + ar[b][C:2 * C] + _dot3(prb_ref[b], _head_stack(u_full, H))
            upd = _dot3(jnp.concatenate([u_full, v_ref[b, sl, :]], axis=0).T,
                        jnp.concatenate([bt_ref[b, sl, :], kt_ref[b, sl, :]], axis=0))
            wl = wl_ref[b, pl.ds(c0 + C - 1, 1), :]
            s_ref[b] = (s0[b] + jnp.where(bd, upd, 0.0)) * wl

    bufs_a, bufs_b = chunk_bufs[0:4], chunk_bufs[4:8]
    prep(0, bufs_a)

    def pair_body(k, carry):
        ci = 2 * k
        prep(ci + 1, bufs_b)
        seq(ci, bufs_a)
        prep(jnp.minimum(ci + 2, n_chunks - 1), bufs_a)
        seq(ci + 1, bufs_b)
        return carry

    lax.fori_loop(0, n_chunks // 2, pair_body, 0)

    for b in range(nb):
        y = y_ref[b]
        mean = _dot_01(y, bo, 2) * (1.0 / HEAD_DIM)
        d = y - mean
        var = _dot_01(d * d, bo, 2) * (1.0 / HEAD_DIM)
        yn = d * lax.rsqrt(var + RWKV_GN_EPS) * row(_RV_LNG) + row(_RV_LNB)
        bonus = _dot_01(bon_ref[b], bo, 2) * v_ref[b]
        o_ref[b] = (yn + bonus) * g_ref[b]


def _rwkv(rw, prm, l):
    B, S, _ = rw.shape
    ts = RWKV_TS
    full = lambda a: pl.BlockSpec(a.shape, lambda s: (0,) * a.ndim)
    args = [prm["rwkv_mu"][l], prm["rwkv_vec"][l], prm["rwkv_w2"][l], prm["rwkv_a2"][l], prm["rwkv_g2"][l],
            prm["rwkv_lt"], prm["block_ones"]]
    big = lambda: pltpu.VMEM((B, ts, RWKV_W), F32)
    return pl.pallas_call(
        _rwkv_kernel,
        grid=(S // ts,),
        in_specs=[pl.BlockSpec((B, ts, 1024), lambda s: (0, s, 0))] + [full(a) for a in args],
        out_specs=pl.BlockSpec((B, ts, RWKV_W), lambda s: (0, s, 0)),
        out_shape=jax.ShapeDtypeStruct((B, S, RWKV_W), F32),
        scratch_shapes=[pltpu.VMEM((B, 8, 1024), F32), pltpu.VMEM((B, RWKV_W, RWKV_W), F32)]
                       + [big() for _ in range(9)]
                       + [pltpu.VMEM((B, RWKV_C, RWKV_W), F32) for _ in range(8)],
        compiler_params=_cparams(("arbitrary",)),
        name="rwkv",
    )(rw, *args)


def _ret_kernel(p_ref, cos_ref, sin_ref, dm_ref, qd_ref, kd_ref, dmat_ref, bo_ref, gn_ref, o_ref, r_ref):
    C = p_ref.shape[1]
    W = RET_W
    H = RET_HEADS

    @pl.when(pl.program_id(1) == 0)
    def _():
        r_ref[...] = jnp.zeros_like(r_ref)

    p = p_ref[0]
    cos, sin = cos_ref[...], sin_ref[...]
    first_half = (_iota((C, W), 1) % HEAD_DIM) < (HEAD_DIM // 2)

    def rope(x):
        partner = jnp.where(first_half, pltpu.roll(x, W - HEAD_DIM // 2, 1), pltpu.roll(x, HEAD_DIM // 2, 1))
        return x * cos + partner * sin

    q = rope(p[:, 0:W])
    k = rope(p[:, W:2 * W]) * (HEAD_DIM ** -0.5)
    v = p[:, 2 * W:3 * W]
    g = p[:, 3 * W:4 * W]
    state = r_ref[...]
    scores = _dot_nt(q, _head_stack(k, H)) * dm_ref[...]
    y = _dot(scores, _head_stack(v, H)) + _dot(q * qd_ref[...], state)
    kv = _dot((k * kd_ref[...]).T, v)
    bd = (_iota((W, W), 0) // HEAD_DIM) == (_iota((W, W), 1) // HEAD_DIM)
    r_ref[...] = state * dmat_ref[...] + jnp.where(bd, kv, 0.0)

    bo = bo_ref[...]
    mean = _dot_01(y, bo, 2) * (1.0 / HEAD_DIM)
    d = y - mean
    var = _dot_01(d * d, bo, 2) * (1.0 / HEAD_DIM)
    o_ref[0] =d * lax.rsqrt(var + RET_GN_EPS) * gn_ref[...] * (g * _sigmoid(g))


def _retention(rt, prm, l):
    B, S, _ = rt.shape
    C = RET_C
    cst = lambda a: pl.BlockSpec(a.shape, lambda b, s: (0,) * a.ndim)
    consts = [prm["ret_dmask"], prm["ret_qdec"], prm["ret_kdec"], prm["ret_dmat"], prm["block_ones"],
              prm["ret_gn"][l]]
    return pl.pallas_call(
        _ret_kernel,
        grid=(B, S // C),
        in_specs=[pl.BlockSpec((1, C, 1024), lambda b, s: (b, s, 0)),
                  pl.BlockSpec((C, RET_W), lambda b, s: (s, 0)),
                  pl.BlockSpec((C, RET_W), lambda b, s: (s, 0))] + [cst(a) for a in consts],
        out_specs=pl.BlockSpec((1, C, RET_W), lambda b, s: (b, s, 0)),
        out_shape=jax.ShapeDtypeStruct((B, S, RET_W), F32),
        scratch_shapes=[pltpu.VMEM((RET_W, RET_W), F32)],
        compiler_params=_cparams(("arbitrary", "arbitrary")),
        name="retention",
    )(rt, prm["ret_cos"], prm["ret_sin"], *consts)


def _ret_tables(S):
    C, H, d = RET_C, RET_HEADS, HEAD_DIM
    half = d // 2
    inv = ROPE_BASE ** (-jnp.arange(half, dtype=F32) / half)
    ang = jnp.arange(S, dtype=F32)[:, None] * inv[None, :]
    cos, sin = jnp.cos(ang), jnp.sin(ang)
    cos_t = jnp.tile(jnp.concatenate([cos, cos], axis=1), (1, H))
    sin_t = jnp.tile(jnp.concatenate([-sin, sin], axis=1), (1, H))
    log_gamma = jnp.log1p(-jnp.exp2(-5.0 - jnp.arange(H, dtype=F32)))
    idx = jnp.arange(C, dtype=F32)
    diff = idx[:, None] - idx[None, :]
    dmask = jnp.where(diff >= 0, jnp.exp(jnp.maximum(diff, 0.0) * log_gamma[:, None, None]), 0.0)
    dmask = jnp.transpose(dmask, (1, 0, 2)).reshape(C, H * C)
    rep = lambda a: jnp.repeat(a, d, axis=1)
    kdec = rep(jnp.exp((C - 1.0 - idx)[:, None] * log_gamma[None, :]))
    qdec = rep(jnp.exp((idx + 1.0)[:, None] * log_gamma[None, :]))
    hid = np.arange(H * d) // d
    bd = jnp.asarray(hid[:, None] == hid[None, :])
    dmat = jnp.where(bd, jnp.repeat(jnp.exp(C * log_gamma), d)[:, None], 0.0)
    return {"ret_cos": cos_t, "ret_sin": sin_t, "ret_dmask": dmask, "ret_qdec": qdec, "ret_kdec": kdec,
            "ret_dmat": dmat}


def _layer_norm(z, g, b):
    mu = jnp.mean(z, axis=1, keepdims=True)
    d = z - mu
    var = jnp.mean(d * d, axis=1, keepdims=True)
    return d * lax.rsqrt(var + LN_EPS) * g + b


ROW_TILES = D_MODEL // LANES


def _out_kernel(yf_ref, yr_ref, yt_ref, x_ref, w_ref, g_ref, b_ref, wr_ref, br_ref,
                x1_ref, info_ref, info_t_ref):
    tm = x_ref.shape[0]
    mixed = (_dot(yf_ref[...], w_ref[0:FOX_W, :])
             + _dot(yr_ref[...].astype(BF16), w_ref[FOX_W:FOX_W + RWKV_W, :])
             + _dot(yt_ref[...].astype(BF16), w_ref[FOX_W + RWKV_W:, :]))
    x1 = _layer_norm(ALPHA * x_ref[...] + mixed, g_ref[...], b_ref[...])
    x1_ref[...] = x1

    logits = _dot3(x1, wr_ref[...]) + br_ref[...]
    lane = _iota((tm, LANES), 1).astype(F32)
    ninf = -jnp.inf
    first = lambda hit: jnp.min(jnp.where(hit, lane, float(LANES)), axis=1, keepdims=True)
    gl = jnp.where(lane < N_GROUPS, logits, ninf)
    gmax = jnp.max(gl, axis=1, keepdims=True)
    grp = first(gl == gmax)
    g_gate = 1.0 / jnp.sum(jnp.exp(gl - gmax), axis=1, keepdims=True)
    lo = N_GROUPS + grp * EXPERTS_PER_GROUP
    el = jnp.where(jnp.logical_and(lane >= lo, lane < lo + EXPERTS_PER_GROUP), logits, ninf)
    v1 = jnp.max(el, axis=1, keepdims=True)
    i1 = first(el == v1)
    el2 = jnp.where(lane == i1, ninf, el)
    v2 = jnp.max(el2, axis=1, keepdims=True)
    i2 = first(el2 == v2)
    e = jnp.exp(v2 - v1)
    den = 1.0 / (1.0 + e)
    info = jnp.where(lane == 0, i1 - N_GROUPS, jnp.where(lane == 1, i2 - N_GROUPS,
           jnp.where(lane == 2, g_gate * den, jnp.where(lane == 3, g_gate * e * den, 0.0))))
    info_ref[...] = info
    info_t_ref[...] = info.T[0:8, :]


def _out_proj(yf, yr, yt, x, prm, l):
    T = x.shape[0]
    tm = OUT_TM
    rows = lambda w: pl.BlockSpec((tm, w), lambda i: (i, 0))
    cst = lambda a: pl.BlockSpec(a.shape, lambda i: (0,) * a.ndim)
    consts = [prm["w_out"][l], prm["ln1_g"][l], prm["ln1_b"][l], prm["w_router"][l], prm["b_router"][l]]
    return pl.pallas_call(
        _out_kernel,
        grid=(T // tm,),
        in_specs=[rows(FOX_W), rows(RWKV_W), rows(RET_W), rows(D_MODEL)] + [cst(a) for a in consts],
        out_specs=[rows(D_MODEL), rows(LANES), pl.BlockSpec((8, tm), lambda i: (0, i))],
        out_shape=[jax.ShapeDtypeStruct((T, D_MODEL), F32), jax.ShapeDtypeStruct((T, LANES), F32),
                   jax.ShapeDtypeStruct((8, T), F32)],
        compiler_params=_cparams(("arbitrary",)),
        name="out_proj",
    )(yf, yr, yt, x, *consts)


def _rank_kernel(it_ref, tri_ref, rk_ref, cnt_ref, carry_ref):
    tm = it_ref.shape[1]

    @pl.when(pl.program_id(0) == 0)
    def _():
        carry_ref[...] = jnp.zeros_like(carry_ref)

    sub = _iota((N_EXPERTS, tm), 0).astype(F32)
    oh1 = sub == it_ref[0:1, :]
    oh2 = sub == it_ref[1:2, :]
    oh = jnp.where(jnp.logical_or(oh1, oh2), 1.0, 0.0)
    before = _dot(oh.astype(BF16), tri_ref[...]) + carry_ref[:, 0:1]
    r1 = jnp.sum(jnp.where(oh1, before, 0.0), axis=0, keepdims=True)
    r2 = jnp.sum(jnp.where(oh2, before, 0.0), axis=0, keepdims=True)
    rk_ref[...] = jnp.concatenate([r1, r2, jnp.zeros((6, tm), F32)], axis=0)
    carry_ref[...] = carry_ref[...] + jnp.sum(oh, axis=1, keepdims=True)
    cnt_ref[...] = carry_ref[...]


def _rank(info_t, tri):
    T = info_t.shape[1]
    tm = RANK_TM
    return pl.pallas_call(
        _rank_kernel,
        grid=(T // tm,),
        in_specs=[pl.BlockSpec((8, tm), lambda i: (0, i)), pl.BlockSpec((tm, tm), lambda i: (0, 0))],
        out_specs=[pl.BlockSpec((8, tm), lambda i: (0, i)), pl.BlockSpec((N_EXPERTS, LANES), lambda i: (0, 0))],
        out_shape=[jax.ShapeDtypeStruct((8, T), F32), jax.ShapeDtypeStruct((N_EXPERTS, LANES), F32)],
        scratch_shapes=[pltpu.VMEM((N_EXPERTS, LANES), F32)],
        compiler_params=_cparams(("arbitrary",)),
        name="rank",
    )(info_t, tri)


def _row_copy(src_hbm, row, dst, r, sem):
    return pltpu.make_async_copy(src_hbm.at[row], dst.at[:, r, :], sem)


def _expert_kernel(be_ref, nu_ref, src0_ref, src1_ref, x_hbm, w1_ref, w3_ref, w2_ref, y_ref,
                   xbuf, sem, w1b, w3b, w2b):
    j = pl.program_id(0)
    tm = xbuf.shape[2]
    nu = nu_ref[0]
    slot = j % 2

    def gather(src_ref, s):
        for r in range(tm):
            _row_copy(x_hbm, src_ref[0, 0, r], xbuf.at[s], r, sem.at[s]).start(priority=r % 2)

    def drain(s):
        def body(r, c):
            _row_copy(x_hbm, 0, xbuf.at[s], r, sem.at[s]).wait()
            return c
        lax.fori_loop(0, tm, body, 0, unroll=8)

    @pl.when(j == 0)
    def _():
        gather(src0_ref, 0)

    @pl.when(j < nu)
    def _():
        drain(slot)
        gather(src1_ref, 1 - slot)

    @pl.when(jnp.logical_and(j < nu, jnp.logical_or(j == 0, be_ref[j] != be_ref[jnp.maximum(j - 1, 0)])))
    def _():
        w1b[...] = w1_ref[0].astype(BF16)
        w3b[...] = w3_ref[0].astype(BF16)
        w2b[...] = w2_ref[0].astype(BF16)

    @pl.when(j < nu)
    def _():
        x = jnp.concatenate([xbuf[slot, c] for c in range(ROW_TILES)], axis=1).astype(BF16)
        a = _dot(x, w1b[...])
        h = (a * _sigmoid(a)) * _dot(x, w3b[...])
        y = _dot(h.astype(BF16), w2b[...])
        for c in range(ROW_TILES):
            y_ref[:, c, :] = y[:, c * LANES:(c + 1) * LANES]

    @pl.when(j == nu)
    def _():
        drain(slot)

    @pl.when(j >= nu)
    def _():
        y_ref[...] = jnp.zeros_like(y_ref)


def _experts(block_expert, n_used, src_tok, x1, w1, w3, w2):
    n_blk = block_expert.shape[0]
    tm = MOE_TM
    src_tok = src_tok.reshape(n_blk, 1, tm)
    grid_spec = pltpu.PrefetchScalarGridSpec(
        num_scalar_prefetch=2,
        grid=(n_blk,),
        in_specs=[pl.BlockSpec((1, 1, tm), lambda j, be, nu: (0, 0, 0), memory_space=pltpu.SMEM),
                  pl.BlockSpec((1, 1, tm), lambda j, be, nu: (jnp.minimum(j + 1, n_blk - 1), 0, 0),
                               memory_space=pltpu.SMEM),
                  pl.BlockSpec(memory_space=pl.ANY),
                  pl.BlockSpec((1, D_MODEL, D_EXPERT), lambda j, be, nu: (be[j], 0, 0)),
                  pl.BlockSpec((1, D_MODEL, D_EXPERT), lambda j, be, nu: (be[j], 0, 0)),
                  pl.BlockSpec((1, D_EXPERT, D_MODEL), lambda j, be, nu: (be[j], 0, 0))],
        out_specs=pl.BlockSpec((tm, ROW_TILES, LANES), lambda j, be, nu: (j, 0, 0)),
        scratch_shapes=[pltpu.VMEM((2, ROW_TILES, tm, LANES), F32), pltpu.SemaphoreType.DMA((2,)),
                        pltpu.VMEM((D_MODEL, D_EXPERT), BF16), pltpu.VMEM((D_MODEL, D_EXPERT), BF16),
                        pltpu.VMEM((D_EXPERT, D_MODEL), BF16)],
    )
    return pl.pallas_call(
        _expert_kernel,
        grid_spec=grid_spec,
        out_shape=jax.ShapeDtypeStruct((n_blk * tm, ROW_TILES, LANES), F32),
        compiler_params=_cparams(("arbitrary",)),
        name="experts",
    )(block_expert, n_used, src_tok, src_tok, x1, w1, w3, w2)


def _combine_kernel(dest0_ref, dest1_ref, y_hbm, info_ref, x_ref, g_ref, b_ref, o_ref, ybuf, sem):
    i = pl.program_id(0)
    tm = x_ref.shape[0]
    slot = i % 2

    def gather(dest_ref, s):
        for r in range(2 * tm):
            _row_copy(y_hbm, dest_ref[0, 0, r], ybuf.at[s], r, sem.at[s]).start(priority=r % 2)

    @pl.when(i == 0)
    def _():
        gather(dest0_ref, 0)

    @pl.when(i + 1 < pl.num_programs(0))
    def _():
        gather(dest1_ref, 1 - slot)

    def drain(r, c):
        _row_copy(y_hbm, 0, ybuf.at[slot], r, sem.at[slot]).wait()
        return c
    lax.fori_loop(0, 2 * tm, drain, 0, unroll=8)
    info = info_ref[...]
    rows = lambda lo: jnp.concatenate([ybuf[slot, c, pl.ds(lo, tm), :] for c in range(ROW_TILES)], axis=1)
    moe = info[:, 2:3] * rows(0) + info[:, 3:4] * rows(tm)
    o_ref[...] = _layer_norm(ALPHA * x_ref[...] + moe, g_ref[...], b_ref[...])


def _combine(dest, y_pad, info, x1, g, b):
    T = x1.shape[0]
    tm = COMB_TM
    cst = lambda a: pl.BlockSpec(a.shape, lambda i: (0,) * a.ndim)
    n = T // tm
    return pl.pallas_call(
        _combine_kernel,
        grid=(n,),
        in_specs=[pl.BlockSpec((1, 1, 2 * tm), lambda i: (0, 0, 0), memory_space=pltpu.SMEM),
                  pl.BlockSpec((1, 1, 2 * tm), lambda i: (jnp.minimum(i + 1, n - 1), 0, 0),
                               memory_space=pltpu.SMEM),
                  pl.BlockSpec(memory_space=pl.ANY),
                  pl.BlockSpec((tm, LANES), lambda i: (i, 0)),
                  pl.BlockSpec((tm, D_MODEL), lambda i: (i, 0)), cst(g), cst(b)],
        out_specs=pl.BlockSpec((tm, D_MODEL), lambda i: (i, 0)),
        out_shape=jax.ShapeDtypeStruct((T, D_MODEL), F32),
        scratch_shapes=[pltpu.VMEM((2, ROW_TILES, 2 * tm, LANES), F32), pltpu.SemaphoreType.DMA((2,))],
        compiler_params=_cparams(("arbitrary",)),
        name="combine",
    )(dest, dest, y_pad, info, x1, g, b)


def _invert_kernel(dest_ref, src_ref):
    n = src_ref.shape[0]
    T = dest_ref.shape[0] // 2

    def zero(i, c):
        src_ref[i] = 0
        return c
    lax.fori_loop(0, n, zero, 0, unroll=8)

    def put(t, c):
        src_ref[dest_ref[t]] = t
        src_ref[dest_ref[T + t]] = t
        return c
    lax.fori_loop(0, T, put, 0, unroll=8)


def _invert(dest_flat, n_rows):
    smem = pl.BlockSpec(memory_space=pltpu.SMEM)
    return pl.pallas_call(
        _invert_kernel,
        in_specs=[smem],
        out_specs=smem,
        out_shape=jax.ShapeDtypeStruct((n_rows,), jnp.int32),
        name="invert",
    )(dest_flat)


def _moe(x1, x1t, info, info_t, prm, l):
    T = x1.shape[0]
    tm = MOE_TM
    n_blk = (2 * T) // tm + N_EXPERTS
    ranks, cnt = _rank(info_t, prm["tri_rank"])
    counts = cnt[:, 0].astype(jnp.int32)
    padded = ((counts + tm - 1) // tm) * tm
    pad_end = jnp.cumsum(padded)
    pad_off = pad_end - padded
    eid = info_t[0:2].astype(jnp.int32)
    e_iota = jnp.arange(N_EXPERTS, dtype=jnp.int32)
    off = jnp.sum(jnp.where(eid[:, None, :] == e_iota[None, :, None], pad_off[None, :, None], 0), axis=1)
    dest = off + ranks[0:2].astype(jnp.int32)
    src_tok = _invert(dest.reshape(-1), n_blk * tm)
    blk_start = jnp.arange(n_blk, dtype=jnp.int32) * tm
    block_expert = jnp.minimum(jnp.sum((pad_end[None, :] <= blk_start[:, None]).astype(jnp.int32), axis=1),
                               N_EXPERTS - 1)
    n_used = (pad_end[-1] // tm).astype(jnp.int32).reshape(1)
    block_expert = jnp.where(jnp.arange(n_blk) < n_used[0], block_expert,
                             block_expert[jnp.maximum(n_used[0] - 1, 0)])
    flat = lambda w: w.reshape((-1,) + w.shape[2:])
    y_pad = _experts(block_expert + l * N_EXPERTS, n_used, src_tok, x1t,
                     flat(prm["moe_w1"]), flat(prm["moe_w3"]), flat(prm["moe_w2"]))
    ct = COMB_TM
    dest_blk = dest.reshape(2, T // ct, ct).transpose(1, 0, 2).reshape(T // ct, 1, 2 * ct)
    return _combine(dest_blk, y_pad, info, x1, prm["ln2_g"][l], prm["ln2_b"][l])


def _prep_params(p):
    L = p["w_in"].shape[0]
    w = p["w_in"]
    fg = jnp.pad(w[:, :, 1536:1544], ((0, 0), (0, 0), (0, LANES - FOX_HEADS)))
    w_in = jnp.concatenate([w[:, :, 0:1536], fg, w[:, :, 1544:2568], w[:, :, 2568:3592]], axis=-1)
    out = {
        "w_in": w_in.astype(BF16),
        "fbias": jnp.pad(p["fox_fgate_bias"], ((0, 0), (0, LANES - FOX_HEADS)))[:, None, :],
        "tri_proj": jnp.tril(jnp.ones((PROJ_TM, PROJ_TM), BF16)),
    }
    vec = jnp.stack([p["rwkv_w0"], p["rwkv_a0"], p["rwkv_k_k"], p["rwkv_k_a"],
                     p["rwkv_r_k"].reshape(L, RWKV_W), p["rwkv_ln_g"], p["rwkv_ln_b"],
                     jnp.zeros((L, RWKV_W), F32)], axis=1)
    ti = np.arange(RWKV_TS)
    lt = ((ti[:, None] // RWKV_C == ti[None, :] // RWKV_C) & (ti[None, :] <= ti[:, None])).astype(np.float32)
    hi = np.arange(RWKV_W) // HEAD_DIM
    out.update({
        "rwkv_mu": jnp.concatenate([p["rwkv_mu_rkv"].reshape(L, 3 * RWKV_W), p["rwkv_mu_lora"]], axis=-1)[:, None, :],
        "rwkv_vec": vec,
        "rwkv_w2": p["rwkv_w2"], "rwkv_a2": p["rwkv_a2"], "rwkv_g2": p["rwkv_g2"],
        "rwkv_lt": jnp.asarray(lt, dtype=BF16),
        "block_ones": jnp.asarray((hi[:, None] == hi[None, :]).astype(np.float32), dtype=BF16),
    })
    out.update(_ret_tables(p["x"].shape[1]))
    out["ret_gn"] = p["ret_gn_g"][:, None, :]
    pad_r = LANES - N_GROUPS - N_EXPERTS
    ri = np.arange(RANK_TM)
    out.update({
        "w_out": p["w_out"].astype(BF16),
        "ln1_g": p["ln1_g"][:, None, :], "ln1_b": p["ln1_b"][:, None, :],
        "ln2_g": p["ln2_g"][:, None, :], "ln2_b": p["ln2_b"][:, None, :],
        "w_router": jnp.pad(jnp.concatenate([p["moe_w_group"], p["moe_w_expert"]], axis=-1),
                            ((0, 0), (0, 0), (0, pad_r))),
        "b_router": jnp.pad(jnp.concatenate([p["moe_b_group"], p["moe_b_expert"]], axis=-1),
                            ((0, 0), (0, pad_r)))[:, None, :],
        "tri_rank": jnp.asarray((ri[:, None] < ri[None, :]).astype(np.float32), dtype=BF16),
        "moe_w1": p["moe_w1"], "moe_w3": p["moe_w3"], "moe_w2": p["moe_w2"],
    })
    return out


def kernel(x, w_in, fox_fgate_bias, rwkv_mu_rkv, rwkv_mu_lora, rwkv_w0, rwkv_w2, rwkv_a0, rwkv_a2, rwkv_g2,
           rwkv_k_k, rwkv_k_a, rwkv_r_k, rwkv_ln_g, rwkv_ln_b, ret_gn_g, w_out, ln1_g, ln1_b, ln2_g, ln2_b,
           moe_w_group, moe_b_group, moe_w_expert, moe_b_expert, moe_w1, moe_w3, moe_w2):
    prm = _prep_params(dict(
        x=x, w_in=w_in, fox_fgate_bias=fox_fgate_bias, rwkv_mu_rkv=rwkv_mu_rkv, rwkv_mu_lora=rwkv_mu_lora,
        rwkv_w0=rwkv_w0, rwkv_w2=rwkv_w2, rwkv_a0=rwkv_a0, rwkv_a2=rwkv_a2, rwkv_g2=rwkv_g2,
        rwkv_k_k=rwkv_k_k, rwkv_k_a=rwkv_k_a, rwkv_r_k=rwkv_r_k, rwkv_ln_g=rwkv_ln_g, rwkv_ln_b=rwkv_ln_b,
        ret_gn_g=ret_gn_g, w_out=w_out, ln1_g=ln1_g, ln1_b=ln1_b, ln2_g=ln2_g, ln2_b=ln2_b,
        moe_w_group=moe_w_group, moe_b_group=moe_b_group, moe_w_expert=moe_w_expert,
        moe_b_expert=moe_b_expert, moe_w1=moe_w1, moe_w3=moe_w3, moe_w2=moe_w2))
    B, S, D = x.shape
    T = B * S
    for l in range(w_in.shape[0]):
        qa, ka, va, rw, rt, stats = _project(x, prm["w_in"][l], prm["fbias"][l], prm["tri_proj"])
        y_fox = _fox_attention(qa, ka, va, stats)
        y_rwkv = _rwkv(rw, prm, l)
        y_ret = _retention(rt, prm, l)
        x1, info, info_t = _out_proj(y_fox.reshape(T, FOX_W), y_rwkv.reshape(T, RWKV_W),
                                     y_ret.reshape(T, RET_W), x.reshape(T, D), prm, l)
        x = _moe(x1, x1.reshape(T, ROW_TILES, LANES), info, info_t, prm, l).reshape(B, S, D)
    return x
```

```python
import functools
import math

import jax
import jax.numpy as jnp
import numpy as np
from jax import lax
from jax.experimental import pallas as pl
from jax.experimental.pallas import tpu as pltpu

F32 = jnp.float32
BF16 = jnp.bfloat16
HIGHEST = lax.Precision.HIGHEST

D_MODEL = 1024
DEPTH = 4
HEAD_DIM = 64
FOX_W, RWKV_W, RET_W = 512, 256, 256
FOX_HEADS, RWKV_HEADS, RET_HEADS = 8, 4, 4
W_LORA, A_LORA, G_LORA = 64, 64, 128
ROPE_BASE = 10000.0
N_GROUPS, EXPERTS_PER_GROUP, N_EXPERTS, D_EXPERT = 4, 8, 32, 512
ALPHA = (2.0 * DEPTH) ** 0.25
LN_EPS = 1e-5
LOG2E = math.log2(math.e)
RWKV_GN_EPS = 64e-5
RET_GN_EPS = 1e-6

LANES = 128
VMEM_LIMIT = 56 * 1024 * 1024

PROJ_TM = 512
ATT_T = 1024
ATT_HEADS = 4
ATT_SKIP_MARGIN = 160.0
RWKV_TS = 512
RWKV_C = 64
RET_C = 256
OUT_TM = 512
RANK_TM = 512
MOE_TM = 512
COMB_TM = 256

P_FOX = 0
P_FG = 1536
P_RWKV = 1664
P_RET = 2688
P_TOT = 3712


def _cparams(sem):
    return pltpu.CompilerParams(dimension_semantics=sem, vmem_limit_bytes=VMEM_LIMIT)


def _dot(a, b, precision=None):
    return jnp.dot(a, b, preferred_element_type=F32, precision=precision)


def _dot_nt(a, b, precision=None):
    return lax.dot_general(a, b, (((1,), (1,)), ((), ())), preferred_element_type=F32,
                           precision=precision)


def _split(a, n):
    parts, r = [], a
    for i in range(n):
        h = r.astype(BF16)
        parts.append(h)
        if i + 1 < n:
            r = r - h.astype(F32)
    return parts


def _dot3(a, b, nt=False):
    d = _dot_nt if nt else _dot
    a1, a2 = _split(a, 2)
    b1, b2 = _split(b, 2)
    return (d(a1, b2) + d(a2, b1)) + d(a1, b1)


def _dot_01(a, m, n=3, left=False):
    d = (lambda p: _dot(m, p)) if left else (lambda p: _dot(p, m))
    parts = _split(a, n)
    acc = d(parts[-1])
    for p in parts[-2::-1]:
        acc = acc + d(p)
    return acc


def _log_sigmoid(z):
    return -(jnp.maximum(-z, 0.0) + jnp.log(1.0 + jnp.exp(-jnp.abs(z))))


def _sigmoid(z):
    return 1.0 / (1.0 + jnp.exp(-z))


def _iota(shape, dim):
    return lax.broadcasted_iota(jnp.int32, shape, dim)


def _proj_kernel(x_ref, w_ref, fb_ref, tri_ref, qa_ref, ka_ref, va_ref, rw_ref, rt_ref, stat_ref, carry_ref):
    tm = x_ref.shape[1]

    @pl.when(pl.program_id(1) == 0)
    def _():
        carry_ref[...] = jnp.zeros_like(carry_ref)

    xb = x_ref[0].astype(BF16)
    p_fox = _dot(xb, w_ref[:, P_FOX:P_FG])
    z = _dot(xb, w_ref[:, P_FG:P_RWKV]) + fb_ref[...]
    rw_ref[0] = _dot(xb, w_ref[:, P_RWKV:P_RET])
    rt_ref[0] = _dot(xb, w_ref[:, P_RET:P_TOT])

    c = _dot_01(_log_sigmoid(z), tri_ref[...], 3, left=True) + carry_ref[0:1, :]
    carry_ref[...] = jnp.broadcast_to(c[tm - 1:tm, :], carry_ref.shape)

    lane = _iota((tm, LANES), 1)
    for h in range(FOX_HEADS):
        cb = jnp.broadcast_to(c[:, h:h + 1], (tm, LANES)) * LOG2E
        hi = cb.astype(BF16).astype(F32)
        r1 = cb - hi
        mid = r1.astype(BF16).astype(F32)
        lo = r1 - mid
        tile = (h // 2) * LANES

        def head_lanes(base):
            t = p_fox[:, base + tile: base + tile + LANES]
            return pltpu.roll(t, 64, 1) if h % 2 else t

        q = head_lanes(0) * (LOG2E * HEAD_DIM ** -0.5)
        k = head_lanes(FOX_W)
        v = head_lanes(2 * FOX_W)
        qa = jnp.where(lane < 64, q, jnp.where(lane == 64, hi, jnp.where(lane == 65, mid,
             jnp.where(lane == 66, lo, jnp.where(lane < 70, 1.0, 0.0)))))
        ka = jnp.where(lane < 64, k, jnp.where(lane < 67, 1.0, jnp.where(lane == 67, -hi,
             jnp.where(lane == 68, -mid, jnp.where(lane == 69, -lo, 0.0)))))
        va = jnp.where(lane < 64, v, jnp.where(lane == 64, 1.0, 0.0))
        qa, ka = qa.astype(BF16), ka.astype(BF16)
        qa_ref[0, h] = qa
        ka_ref[0, h] = ka
        va_ref[0, h] = va.astype(BF16)
        qr = jnp.where(lane < 64, qa.astype(F32), 0.0)
        kr = jnp.where(lane < 64, ka.astype(F32), 0.0)
        col_max = lambda a: jnp.max(jnp.sum(a, axis=1, keepdims=True), axis=0, keepdims=True)
        l1 = _iota((1, LANES), 1)
        stat_ref[0, 0, h:h + 1, :] = jnp.where(
            l1 == 0, col_max(qr * qr), jnp.where(l1 == 1, col_max(kr * kr), jnp.where(
                l1 == 2, cb[0:1, :], jnp.where(l1 == 3, cb[tm - 1:tm, :], 0.0))))


def _project(x, w_bf16, fbias, tri):
    B, S, D = x.shape
    tm = PROJ_TM
    aug = jax.ShapeDtypeStruct((B, FOX_HEADS, S, LANES), BF16)
    wide = jax.ShapeDtypeStruct((B, S, 1024), F32)
    aug_spec = pl.BlockSpec((1, FOX_HEADS, tm, LANES), lambda b, s: (b, 0, s, 0))
    wide_spec = pl.BlockSpec((1, tm, 1024), lambda b, s: (b, s, 0))
    return pl.pallas_call(
        _proj_kernel,
        grid=(B, S // tm),
        in_specs=[pl.BlockSpec((1, tm, D), lambda b, s: (b, s, 0)),
                  pl.BlockSpec((D, P_TOT), lambda b, s: (0, 0)),
                  pl.BlockSpec((1, LANES), lambda b, s: (0, 0)),
                  pl.BlockSpec((tm, tm), lambda b, s: (0, 0))],
        out_specs=[aug_spec, aug_spec, aug_spec, wide_spec, wide_spec,
                   pl.BlockSpec((1, 1, FOX_HEADS, LANES), lambda b, s: (b, s, 0, 0))],
        out_shape=[aug, aug, aug, wide, wide, jax.ShapeDtypeStruct((B, S // tm, FOX_HEADS, LANES), F32)],
        scratch_shapes=[pltpu.VMEM((8, LANES), F32)],
        compiler_params=_cparams(("arbitrary", "arbitrary")),
        name="proj",
    )(x, w_bf16, fbias, tri)


def _fox_kernel(qi_ref, ki_ref, skip_ref, kv_ref, q_ref, k_ref, v_ref, o_ref, m_ref, acc_ref):
    p = pl.program_id(2)
    qi = qi_ref[p]
    ki = ki_ref[p]
    nh, t = q_ref.shape[1], q_ref.shape[2]
    skip = skip_ref[(pl.program_id(0) * pl.num_programs(1) + pl.program_id(1)) * pl.num_programs(2) + p]

    @pl.when(ki == qi)
    def _():
        m_ref[...] = jnp.full_like(m_ref, -jnp.inf)
        acc_ref[...] = jnp.zeros_like(acc_ref)

    def step(masked):
        qk = lambda j: _dot_nt(q_ref[0, j], k_ref[0, j])
        ss = {j: qk(j) for j in range(min(2, nh))}
        for j in range(nh):
            if j + 2 < nh:
                ss[j + 2] = qk(j + 2)
            s = ss.pop(j)
            if masked:
                s = jnp.where(_iota((t, t), 1) <= _iota((t, t), 0), s, -jnp.inf)
            m_prev = m_ref[j]
            m_new = jnp.maximum(m_prev, jnp.max(s, axis=1, keepdims=True))
            alpha = jnp.exp2(m_prev - m_new)
            pm = jnp.exp2(s - jnp.concatenate([m_new] * (t // LANES), axis=1))
            acc_ref[j] = alpha * acc_ref[j] + _dot(pm.astype(BF16), v_ref[0, j])
            m_ref[j] = m_new

    @pl.when(ki == qi)
    def _():
        step(True)

    @pl.when(jnp.logical_and(ki < qi, skip == 0))
    def _():
        step(False)

    @pl.when(ki == 0)
    def _():
        lane = _iota((t, LANES), 1)
        for jp in range(nh // 2):
            a0, a1 = acc_ref[2 * jp], acc_ref[2 * jp + 1]
            o0, o1 = a0 / a0[:, 64:65], a1 / a1[:, 64:65]
            o_ref[0, :, jp * LANES:(jp + 1) * LANES] = jnp.where(
                lane < 64, o0, pltpu.roll(o1, 64, 1)).astype(o_ref.dtype)


def _fox_skip_flags(stats, qi_arr, ki_arr):
    B, n_sub, H, _ = stats.shape
    r = ATT_T // PROJ_TM
    st = stats.reshape(B, n_sub // r, r, H, LANES)
    qn = jnp.sqrt(jnp.max(st[..., 0], axis=2))
    kn = jnp.sqrt(jnp.max(st[..., 1], axis=2))
    c_first = st[:, :, 0, :, 2]
    c_last = st[:, :, r - 1, :, 3]
    qi, ki = jnp.asarray(qi_arr), jnp.asarray(ki_arr)
    bound = qn[:, qi] * kn[:, ki] + c_first[:, qi] - c_last[:, ki]
    dead = bound < -(qn * kn)[:, qi] - ATT_SKIP_MARGIN
    dead = jnp.all(dead.reshape(B, -1, H // ATT_HEADS, ATT_HEADS), axis=-1)
    return jnp.transpose(dead, (0, 2, 1)).astype(jnp.int32).reshape(-1)


def _fox_attention(qa, ka, va, stats):
    B, H, S, _ = qa.shape
    t = ATT_T
    nh = ATT_HEADS
    nq = S // t
    qi_arr = np.concatenate([np.full(i + 1, i, np.int32) for i in range(nq)])
    ki_arr = np.concatenate([np.arange(i, -1, -1, dtype=np.int32) for i in range(nq)])
    n_pairs = qi_arr.shape[0]
    skip = _fox_skip_flags(stats, qi_arr, ki_arr)
    n_groups = H // nh
    pos = jnp.where(skip.reshape(-1, n_pairs) == 0, jnp.arange(n_pairs, dtype=jnp.int32), -1)
    kv_blk = jnp.asarray(ki_arr)[lax.cummax(pos, axis=1)].reshape(-1)
    flat = lambda b, hp, p: (b * n_groups + hp) * n_pairs + p
    grid_spec = pltpu.PrefetchScalarGridSpec(
        num_scalar_prefetch=4,
        grid=(B, n_groups, n_pairs),
        in_specs=[pl.BlockSpec((1, nh, t, LANES), lambda b, hp, p, qi, ki, sk, kv: (b, hp, qi[p], 0)),
                  pl.BlockSpec((1, nh, t, LANES), lambda b, hp, p, qi, ki, sk, kv: (b, hp, kv[flat(b, hp, p)], 0)),
                  pl.BlockSpec((1, nh, t, LANES), lambda b, hp, p, qi, ki, sk, kv: (b, hp, kv[flat(b, hp, p)], 0))],
        out_specs=pl.BlockSpec((1, t, nh * HEAD_DIM), lambda b, hp, p, qi, ki, sk, kv: (b, qi[p], hp)),
        scratch_shapes=[pltpu.VMEM((nh, t, LANES), F32), pltpu.VMEM((nh, t, LANES), F32)],
    )
    return pl.pallas_call(
        _fox_kernel,
        grid_spec=grid_spec,
        out_shape=jax.ShapeDtypeStruct((B, S, FOX_W), BF16),
        compiler_params=_cparams(("arbitrary", "arbitrary", "arbitrary")),
        name="fox_attn",
    )(jnp.asarray(qi_arr), jnp.asarray(ki_arr), skip, kv_blk, qa, ka, va)


_RV_W0, _RV_A0, _RV_KK, _RV_KA, _RV_RK, _RV_LNG, _RV_LNB = range(7)


def _head_stack(x, n_heads):
    lane = _iota(x.shape, 1)
    return jnp.concatenate(
        [jnp.where(lane // HEAD_DIM == h, x, 0.0) for h in range(n_heads)], axis=0)


def _rwkv_kernel(p_ref, mu_ref, vec_ref, w2_ref, a2_ref, g2_ref, lt_ref, bo_ref, o_ref,
                 prev_ref, s_ref, at_ref, rt_ref, bt_ref, kt_ref, v_ref, wl_ref, y_ref, bon_ref, g_ref,
                 *chunk_bufs):
    nb, ts, _ = p_ref.shape
    C = RWKV_C
    H = RWKV_HEADS
    W = RWKV_W

    @pl.when(pl.program_id(0) == 0)
    def _():
        prev_ref[...] = jnp.zeros_like(prev_ref)
        s_ref[...] = jnp.zeros_like(s_ref)

    vec = vec_ref[...]
    row = lambda i: vec[i:i + 1, :]
    bo = bo_ref[...]

    for b in range(nb):
        p = p_ref[b]
        rowi = _iota(p.shape, 0)
        prev = jnp.where(rowi == 0, jnp.broadcast_to(prev_ref[b, 0:1, :], p.shape), pltpu.roll(p, 1, 0))
        prev_ref[b] = jnp.broadcast_to(p[ts - 1:ts, :], prev_ref.shape[1:])
        xm = p + (prev - p) * mu_ref[...]
        r, k, v = xm[:, 0:W], xm[:, W:2 * W], xm[:, 2 * W:3 * W]
        wd = xm[:, 3 * W:3 * W + W_LORA]
        ad = xm[:, 3 * W + W_LORA:3 * W + W_LORA + A_LORA]
        gd = xm[:, 3 * W + W_LORA + A_LORA:]
        lw = row(_RV_W0) + _dot3(jnp.tanh(wd), w2_ref[...])
        logdec = -jnp.exp(_log_sigmoid(lw) - 0.5)
        a = _sigmoid(row(_RV_A0) + _dot3(ad, a2_ref[...]))
        g_ref[b] = _dot3(_sigmoid(gd), g2_ref[...])
        kk = k * row(_RV_KK)
        kk = kk / jnp.maximum(jnp.sqrt(_dot_01(kk * kk, bo, 2)), 1e-12)
        k2 = k * (1.0 + (a - 1.0) * row(_RV_KA))
        cum = _dot_01(logdec, lt_ref[...], 3, left=True)
        wincl = jnp.exp(cum)
        winv = jnp.exp(-cum)
        at_ref[b] = -kk * jnp.exp(cum - logdec)
        rt_ref[b] = r * wincl
        bt_ref[b] = kk * a * winv
        kt_ref[b] = k2 * winv
        v_ref[b] = v
        wl_ref[b] = wincl
        bon_ref[b] = r * k2 * row(_RV_RK)

    li = _iota((C, H * C), 1) % C
    ri = _iota((C, H * C), 0)
    strict = li < ri
    incl = li <= ri
    bd = (_iota((W, W), 0) // HEAD_DIM) == (_iota((W, W), 1) // HEAD_DIM)
    n_chunks = ts // C

    def prep(ci, bufs):
        pab_ref, prb_ref, base_ref, y0_ref = bufs
        sl = pl.ds(pl.multiple_of(ci * C, C), C)
        for b in range(nb):
            at, rt, bt, kt, vv = at_ref[b, sl, :], rt_ref[b, sl, :], bt_ref[b, sl, :], kt_ref[b, sl, :], v_ref[b, sl, :]
            bstack, kstack, vstack = _head_stack(bt, H), _head_stack(kt, H), _head_stack(vv, H)
            pab_ref[b] = jnp.where(strict, _dot3(at, bstack, nt=True), 0.0)
            pak = jnp.where(strict, _dot3(at, kstack, nt=True), 0.0)
            prb_ref[b] = jnp.where(incl, _dot3(rt, bstack, nt=True), 0.0)
            prk = jnp.where(incl, _dot3(rt, kstack, nt=True), 0.0)
            base_ref[b] = _dot3(pak, vstack)
            y0_ref[b] = _dot3(prk, vstack)

    def seq(ci, bufs):
        pab_ref, prb_ref, base_ref, y0_ref = bufs
        c0 = pl.multiple_of(ci * C, C)
        sl = pl.ds(c0, C)
        s0 = [s_ref[b] for b in range(nb)]
        ar = [_dot3(jnp.concatenate([at_ref[b, sl, :], rt_ref[b, sl, :]], axis=0), s0[b], nt=True)
              for b in range(nb)]
        half = C // 2
        lo_lanes = _iota((8, LANES), 1) < HEAD_DIM
        lo_half = _iota((half, LANES), 1) < HEAD_DIM
        u = {}
        for b in range(nb):
            base = base_ref[b] + ar[b][0:C]
            for hp in range(H // 2):
                u[b, hp] = [base[8 * j:8 * j + 8, hp * LANES:(hp + 1) * LANES] for j in range(C // 8)]
        pab = {(b, hp): pab_ref[b, :, hp * LANES:(hp + 1) * LANES] for b in range(nb) for hp in range(H // 2)}

        def solve_half(t0):
            for t in range(t0, t0 + half - 1):
                for key, w in u.items():
                    u_t = jnp.broadcast_to(w[t // 8][t % 8:t % 8 + 1, :], (8, LANES))
                    for j in range((t + 1) // 8, (t0 + half) // 8):
                        rows = pab[key][8 * j:8 * j + 8, :]
                        coef = jnp.where(lo_lanes, jnp.broadcast_to(rows[:, t:t + 1], (8, LANES)),
                                         jnp.broadcast_to(rows[:, HEAD_DIM + t:HEAD_DIM + t + 1], (8, LANES)))
                        w[j] = w[j] + coef * u_t

        solve_half(0)
        for key, w in u.items():
            first = jnp.concatenate(w[0:half // 8], axis=0)
            a_lo = pab[key][half:C, 0:half]
            a_hi = pab[key][half:C, HEAD_DIM:HEAD_DIM + half]
            inc = jnp.where(lo_half, _dot3(a_lo, first), _dot3(a_hi, first))
            for j in range(half // 8, C // 8):
                w[j] = w[j] + inc[8 * j - half:8 * j - half + 8, :]
        solve_half(half)
        for b in range(nb):
            u_full = jnp.concatenate([jnp.concatenate(u[b, hp], axis=0) for hp in range(H // 2)], axis=1)
            y_ref[b, sl, :] = y0_ref[b] + ar[b][C:2 * C] + _dot3(prb_ref[b], _head_stack(u_full, H))
            upd = _dot3(jnp.concatenate([u_full, v_ref[b, sl, :]], axis=0).T,
                        jnp.concatenate([bt_ref[b, sl, :], kt_ref[b, sl, :]], axis=0))
            wl = wl_ref[b, pl.ds(c0 + C - 1, 1), :]
            s_ref[b] = (s0[b] + jnp.where(bd, upd, 0.0)) * wl

    bufs_a, bufs_b = chunk_bufs[0:4], chunk_bufs[4:8]
    prep(0, bufs_a)

    def pair_body(k, carry):
        ci = 2 * k
        prep(ci + 1, bufs_b)
        seq(ci, bufs_a)
        prep(jnp.minimum(ci + 2, n_chunks - 1), bufs_a)
        seq(ci + 1, bufs_b)
        return carry

    lax.fori_loop(0, n_chunks // 2, pair_body, 0)

    for b in range(nb):
        y = y_ref[b]
        mean = _dot_01(y, bo, 2) * (1.0 / HEAD_DIM)
        d = y - mean
        var = _dot_01(d * d, bo, 2) * (1.0 / HEAD_DIM)
        yn = d * lax.rsqrt(var + RWKV_GN_EPS) * row(_RV_LNG) + row(_RV_LNB)
        bonus = _dot_01(bon_ref[b], bo, 2) * v_ref[b]
        o_ref[b] = (yn + bonus) * g_ref[b]


def _rwkv(rw, prm, l):
    B, S, _ = rw.shape
    ts = RWKV_TS
    full = lambda a: pl.BlockSpec(a.shape, lambda s: (0,) * a.ndim)
    args = [prm["rwkv_mu"][l], prm["rwkv_vec"][l], prm["rwkv_w2"][l], prm["rwkv_a2"][l], prm["rwkv_g2"][l],
            prm["rwkv_lt"], prm["block_ones"]]
    big = lambda: pltpu.VMEM((B, ts, RWKV_W), F32)
    return pl.pallas_call(
        _rwkv_kernel,
        grid=(S // ts,),
        in_specs=[pl.BlockSpec((B, ts, 1024), lambda s: (0, s, 0))] + [full(a) for a in args],
        out_specs=pl.BlockSpec((B, ts, RWKV_W), lambda s: (0, s, 0)),
        out_shape=jax.ShapeDtypeStruct((B, S, RWKV_W), F32),
        scratch_shapes=[pltpu.VMEM((B, 8, 1024), F32), pltpu.VMEM((B, RWKV_W, RWKV_W), F32)]
                       + [big() for _ in range(9)]
                       + [pltpu.VMEM((B, RWKV_C, RWKV_W), F32) for _ in range(8)],
        compiler_params=_cparams(("arbitrary",)),
        name="rwkv",
    )(rw, *args)


def _ret_kernel(p_ref, cos_ref, sin_ref, dm_ref, qd_ref, kd_ref, dmat_ref, bo_ref, gn_ref, o_ref, r_ref):
    C = p_ref.shape[1]
    W = RET_W
    H = RET_HEADS

    @pl.when(pl.program_id(1) == 0)
    def _():
        r_ref[...] = jnp.zeros_like(r_ref)

    p = p_ref[0]
    cos, sin = cos_ref[...], sin_ref[...]
    first_half = (_iota((C, W), 1) % HEAD_DIM) < (HEAD_DIM // 2)

    def rope(x):
        partner = jnp.where(first_half, pltpu.roll(x, W - HEAD_DIM // 2, 1), pltpu.roll(x, HEAD_DIM // 2, 1))
        return x * cos + partner * sin

    q = rope(p[:, 0:W])
    k = rope(p[:, W:2 * W]) * (HEAD_DIM ** -0.5)
    v = p[:, 2 * W:3 * W]
    g = p[:, 3 * W:4 * W]
    state = r_ref[...]
    scores = _dot_nt(q, _head_stack(k, H)) * dm_ref[...]
    y = _dot(scores, _head_stack(v, H)) + _dot(q * qd_ref[...], state)
    kv = _dot((k * kd_ref[...]).T, v)
    bd = (_iota((W, W), 0) // HEAD_DIM) == (_iota((W, W), 1) // HEAD_DIM)
    r_ref[...] = state * dmat_ref[...] + jnp.where(bd, kv, 0.0)

    bo = bo_ref[...]
    mean = _dot_01(y, bo, 2) * (1.0 / HEAD_DIM)
    d = y - mean
    var = _dot_01(d * d, bo, 2) * (1.0 / HEAD_DIM)
    o_ref[0] =d * lax.rsqrt(var + RET_GN_EPS) * gn_ref[...] * (g * _sigmoid(g))


def _retention(rt, prm, l):
    B, S, _ = rt.shape
    C = RET_C
    cst = lambda a: pl.BlockSpec(a.shape, lambda b, s: (0,) * a.ndim)
    consts = [prm["ret_dmask"], prm["ret_qdec"], prm["ret_kdec"], prm["ret_dmat"], prm["block_ones"],
              prm["ret_gn"][l]]
    return pl.pallas_call(
        _ret_kernel,
        grid=(B, S // C),
        in_specs=[pl.BlockSpec((1, C, 1024), lambda b, s: (b, s, 0)),
                  pl.BlockSpec((C, RET_W), lambda b, s: (s, 0)),
                  pl.BlockSpec((C, RET_W), lambda b, s: (s, 0))] + [cst(a) for a in consts],
        out_specs=pl.BlockSpec((1, C, RET_W), lambda b, s: (b, s, 0)),
        out_shape=jax.ShapeDtypeStruct((B, S, RET_W), F32),
        scratch_shapes=[pltpu.VMEM((RET_W, RET_W), F32)],
        compiler_params=_cparams(("arbitrary", "arbitrary")),
        name="retention",
    )(rt, prm["ret_cos"], prm["ret_sin"], *consts)


def _ret_tables(S):
    C, H, d = RET_C, RET_HEADS, HEAD_DIM
    half = d // 2
    inv = ROPE_BASE ** (-jnp.arange(half, dtype=F32) / half)
    ang = jnp.arange(S, dtype=F32)[:, None] * inv[None, :]
    cos, sin = jnp.cos(ang), jnp.sin(ang)
    cos_t = jnp.tile(jnp.concatenate([cos, cos], axis=1), (1, H))
    sin_t = jnp.tile(jnp.concatenate([-sin, sin], axis=1), (1, H))
    log_gamma = jnp.log1p(-jnp.exp2(-5.0 - jnp.arange(H, dtype=F32)))
    idx = jnp.arange(C, dtype=F32)
    diff = idx[:, None] - idx[None, :]
    dmask = jnp.where(diff >= 0, jnp.exp(jnp.maximum(diff, 0.0) * log_gamma[:, None, None]), 0.0)
    dmask = jnp.transpose(dmask, (1, 0, 2)).reshape(C, H * C)
    rep = lambda a: jnp.repeat(a, d, axis=1)
    kdec = rep(jnp.exp((C - 1.0 - idx)[:, None] * log_gamma[None, :]))
    qdec = rep(jnp.exp((idx + 1.0)[:, None] * log_gamma[None, :]))
    hid = np.arange(H * d) // d
    bd = jnp.asarray(hid[:, None] == hid[None, :])
    dmat = jnp.where(bd, jnp.repeat(jnp.exp(C * log_gamma), d)[:, None], 0.0)
    return {"ret_cos": cos_t, "ret_sin": sin_t, "ret_dmask": dmask, "ret_qdec": qdec, "ret_kdec": kdec,
            "ret_dmat": dmat}


def _layer_norm(z, g, b):
    mu = jnp.mean(z, axis=1, keepdims=True)
    d = z - mu
    var = jnp.mean(d * d, axis=1, keepdims=True)
    return d * lax.rsqrt(var + LN_EPS) * g + b


ROW_TILES = D_MODEL // LANES


def _out_kernel(yf_ref, yr_ref, yt_ref, x_ref, w_ref, g_ref, b_ref, wr_ref, br_ref,
                x1_ref, info_ref, info_t_ref):
    tm = x_ref.shape[0]
    mixed = (_dot(yf_ref[...], w_ref[0:FOX_W, :])
             + _dot(yr_ref[...].astype(BF16), w_ref[FOX_W:FOX_W + RWKV_W, :])
             + _dot(yt_ref[...].astype(BF16), w_ref[FOX_W + RWKV_W:, :]))
    x1 = _layer_norm(ALPHA * x_ref[...] + mixed, g_ref[...], b_ref[...])
    x1_ref[...] = x1

    logits = _dot3(x1, wr_ref[...]) + br_ref[...]
    lane = _iota((tm, LANES), 1).astype(F32)
    ninf = -jnp.inf
    first = lambda hit: jnp.min(jnp.where(hit, lane, float(LANES)), axis=1, keepdims=True)
    gl = jnp.where(lane < N_GROUPS, logits, ninf)
    gmax = jnp.max(gl, axis=1, keepdims=True)
    grp = first(gl == gmax)
    g_gate = 1.0 / jnp.sum(jnp.exp(gl - gmax), axis=1, keepdims=True)
    lo = N_GROUPS + grp * EXPERTS_PER_GROUP
    el = jnp.where(jnp.logical_and(lane >= lo, lane < lo + EXPERTS_PER_GROUP), logits, ninf)
    v1 = jnp.max(el, axis=1, keepdims=True)
    i1 = first(el == v1)
    el2 = jnp.where(lane == i1, ninf, el)
    v2 = jnp.max(el2, axis=1, keepdims=True)
    i2 = first(el2 == v2)
    e = jnp.exp(v2 - v1)
    den = 1.0 / (1.0 + e)
    info = jnp.where(lane == 0, i1 - N_GROUPS, jnp.where(lane == 1, i2 - N_GROUPS,
           jnp.where(lane == 2, g_gate * den, jnp.where(lane == 3, g_gate * e * den, 0.0))))
    info_ref[...] = info
    info_t_ref[...] = info.T[0:8, :]


def _out_proj(yf, yr, yt, x, prm, l):
    T = x.shape[0]
    tm = OUT_TM
    rows = lambda w: pl.BlockSpec((tm, w), lambda i: (i, 0))
    cst = lambda a: pl.BlockSpec(a.shape, lambda i: (0,) * a.ndim)
    consts = [prm["w_out"][l], prm["ln1_g"][l], prm["ln1_b"][l], prm["w_router"][l], prm["b_router"][l]]
    return pl.pallas_call(
        _out_kernel,
        grid=(T // tm,),
        in_specs=[rows(FOX_W), rows(RWKV_W), rows(RET_W), rows(D_MODEL)] + [cst(a) for a in consts],
        out_specs=[rows(D_MODEL), rows(LANES), pl.BlockSpec((8, tm), lambda i: (0, i))],
        out_shape=[jax.ShapeDtypeStruct((T, D_MODEL), F32), jax.ShapeDtypeStruct((T, LANES), F32),
                   jax.ShapeDtypeStruct((8, T), F32)],
        compiler_params=_cparams(("arbitrary",)),
        name="out_proj",
    )(yf, yr, yt, x, *consts)


def _rank_kernel(it_ref, tri_ref, rk_ref, cnt_ref, carry_ref):
    tm = it_ref.shape[1]

    @pl.when(pl.program_id(0) == 0)
    def _():
        carry_ref[...] = jnp.zeros_like(carry_ref)

    sub = _iota((N_EXPERTS, tm), 0).astype(F32)
    oh1 = sub == it_ref[0:1, :]
    oh2 = sub == it_ref[1:2, :]
    oh = jnp.where(jnp.logical_or(oh1, oh2), 1.0, 0.0)
    before = _dot(oh.astype(BF16), tri_ref[...]) + carry_ref[:, 0:1]
    r1 = jnp.sum(jnp.where(oh1, before, 0.0), axis=0, keepdims=True)
    r2 = jnp.sum(jnp.where(oh2, before, 0.0), axis=0, keepdims=True)
    rk_ref[...] = jnp.concatenate([r1, r2, jnp.zeros((6, tm), F32)], axis=0)
    carry_ref[...] = carry_ref[...] + jnp.sum(oh, axis=1, keepdims=True)
    cnt_ref[...] = carry_ref[...]


def _rank(info_t, tri):
    T = info_t.shape[1]
    tm = RANK_TM
    return pl.pallas_call(
        _rank_kernel,
        grid=(T // tm,),
        in_specs=[pl.BlockSpec((8, tm), lambda i: (0, i)), pl.BlockSpec((tm, tm), lambda i: (0, 0))],
        out_specs=[pl.BlockSpec((8, tm), lambda i: (0, i)), pl.BlockSpec((N_EXPERTS, LANES), lambda i: (0, 0))],
        out_shape=[jax.ShapeDtypeStruct((8, T), F32), jax.ShapeDtypeStruct((N_EXPERTS, LANES), F32)],
        scratch_shapes=[pltpu.VMEM((N_EXPERTS, LANES), F32)],
        compiler_params=_cparams(("arbitrary",)),
        name="rank",
    )(info_t, tri)


def _row_copy(src_hbm, row, dst, r, sem):
    return pltpu.make_async_copy(src_hbm.at[row], dst.at[:, r, :], sem)


def _expert_kernel(be_ref, nu_ref, src0_ref, src1_ref, x_hbm, w1_ref, w3_ref, w2_ref, y_ref,
                   xbuf, sem, w1b, w3b, w2b):
    j = pl.program_id(0)
    tm = xbuf.shape[2]
    nu = nu_ref[0]
    slot = j % 2

    def gather(src_ref, s):
        for r in range(tm):
            _row_copy(x_hbm, src_ref[0, 0, r], xbuf.at[s], r, sem.at[s]).start(priority=r % 2)

    def drain(s):
        def body(r, c):
            _row_copy(x_hbm, 0, xbuf.at[s], r, sem.at[s]).wait()
            return c
        lax.fori_loop(0, tm, body, 0, unroll=8)

    @pl.when(j == 0)
    def _():
        gather(src0_ref, 0)

    @pl.when(j < nu)
    def _():
        drain(slot)
        gather(src1_ref, 1 - slot)

    @pl.when(jnp.logical_and(j < nu, jnp.logical_or(j == 0, be_ref[j] != be_ref[jnp.maximum(j - 1, 0)])))
    def _():
        w1b[...] = w1_ref[0].astype(BF16)
        w3b[...] = w3_ref[0].astype(BF16)
        w2b[...] = w2_ref[0].astype(BF16)

    @pl.when(j < nu)
    def _():
        x = jnp.concatenate([xbuf[slot, c] for c in range(ROW_TILES)], axis=1).astype(BF16)
        a = _dot(x, w1b[...])
        h = (a * _sigmoid(a)) * _dot(x, w3b[...])
        y = _dot(h.astype(BF16), w2b[...])
        for c in range(ROW_TILES):
            y_ref[:, c, :] = y[:, c * LANES:(c + 1) * LANES]

    @pl.when(j == nu)
    def _():
        drain(slot)

    @pl.when(j >= nu)
    def _():
        y_ref[...] = jnp.zeros_like(y_ref)


def _experts(block_expert, n_used, src_tok, x1, w1, w3, w2):
    n_blk = block_expert.shape[0]
    tm = MOE_TM
    src_tok = src_tok.reshape(n_blk, 1, tm)
    grid_spec = pltpu.PrefetchScalarGridSpec(
        num_scalar_prefetch=2,
        grid=(n_blk,),
        in_specs=[pl.BlockSpec((1, 1, tm), lambda j, be, nu: (0, 0, 0), memory_space=pltpu.SMEM),
                  pl.BlockSpec((1, 1, tm), lambda j, be, nu: (jnp.minimum(j + 1, n_blk - 1), 0, 0),
                               memory_space=pltpu.SMEM),
                  pl.BlockSpec(memory_space=pl.ANY),
                  pl.BlockSpec((1, D_MODEL, D_EXPERT), lambda j, be, nu: (be[j], 0, 0)),
                  pl.BlockSpec((1, D_MODEL, D_EXPERT), lambda j, be, nu: (be[j], 0, 0)),
                  pl.BlockSpec((1, D_EXPERT, D_MODEL), lambda j, be, nu: (be[j], 0, 0))],
        out_specs=pl.BlockSpec((tm, ROW_TILES, LANES), lambda j, be, nu: (j, 0, 0)),
        scratch_shapes=[pltpu.VMEM((2, ROW_TILES, tm, LANES), F32), pltpu.SemaphoreType.DMA((2,)),
                        pltpu.VMEM((D_MODEL, D_EXPERT), BF16), pltpu.VMEM((D_MODEL, D_EXPERT), BF16),
                        pltpu.VMEM((D_EXPERT, D_MODEL), BF16)],
    )
    return pl.pallas_call(
        _expert_kernel,
        grid_spec=grid_spec,
        out_shape=jax.ShapeDtypeStruct((n_blk * tm, ROW_TILES, LANES), F32),
        compiler_params=_cparams(("arbitrary",)),
        name="experts",
    )(block_expert, n_used, src_tok, src_tok, x1, w1, w3, w2)


def _combine_kernel(dest0_ref, dest1_ref, y_hbm, info_ref, x_ref, g_ref, b_ref, o_ref, ybuf, sem):
    i = pl.program_id(0)
    tm = x_ref.shape[0]
    slot = i % 2

    def gather(dest_ref, s):
        for r in range(2 * tm):
            _row_copy(y_hbm, dest_ref[0, 0, r], ybuf.at[s], r, sem.at[s]).start(priority=r % 2)

    @pl.when(i == 0)
    def _():
        gather(dest0_ref, 0)

    @pl.when(i + 1 < pl.num_programs(0))
    def _():
        gather(dest1_ref, 1 - slot)

    def drain(r, c):
        _row_copy(y_hbm, 0, ybuf.at[slot], r, sem.at[slot]).wait()
        return c
    lax.fori_loop(0, 2 * tm, drain, 0, unroll=8)
    info = info_ref[...]
    rows = lambda lo: jnp.concatenate([ybuf[slot, c, pl.ds(lo, tm), :] for c in range(ROW_TILES)], axis=1)
    moe = info[:, 2:3] * rows(0) + info[:, 3:4] * rows(tm)
    o_ref[...] = _layer_norm(ALPHA * x_ref[...] + moe, g_ref[...], b_ref[...])


def _combine(dest, y_pad, info, x1, g, b):
    T = x1.shape[0]
    tm = COMB_TM
    cst = lambda a: pl.BlockSpec(a.shape, lambda i: (0,) * a.ndim)
    n = T // tm
    return pl.pallas_call(
        _combine_kernel,
        grid=(n,),
        in_specs=[pl.BlockSpec((1, 1, 2 * tm), lambda i: (0, 0, 0), memory_space=pltpu.SMEM),
                  pl.BlockSpec((1, 1, 2 * tm), lambda i: (jnp.minimum(i + 1, n - 1), 0, 0),
                               memory_space=pltpu.SMEM),
                  pl.BlockSpec(memory_space=pl.ANY),
                  pl.BlockSpec((tm, LANES), lambda i: (i, 0)),
                  pl.BlockSpec((tm, D_MODEL), lambda i: (i, 0)), cst(g), cst(b)],
        out_specs=pl.BlockSpec((tm, D_MODEL), lambda i: (i, 0)),
        out_shape=jax.ShapeDtypeStruct((T, D_MODEL), F32),
        scratch_shapes=[pltpu.VMEM((2, ROW_TILES, 2 * tm, LANES), F32), pltpu.SemaphoreType.DMA((2,))],
        compiler_params=_cparams(("arbitrary",)),
        name="combine",
    )(dest, dest, y_pad, info, x1, g, b)


def _invert_kernel(dest_ref, src_ref):
    n = src_ref.shape[0]
    T = dest_ref.shape[0] // 2

    def zero(i, c):
        src_ref[i] = 0
        return c
    lax.fori_loop(0, n, zero, 0, unroll=8)

    def put(t, c):
        src_ref[dest_ref[t]] = t
        src_ref[dest_ref[T + t]] = t
        return c
    lax.fori_loop(0, T, put, 0, unroll=8)


def _invert(dest_flat, n_rows):
    smem = pl.BlockSpec(memory_space=pltpu.SMEM)
    return pl.pallas_call(
        _invert_kernel,
        in_specs=[smem],
        out_specs=smem,
        out_shape=jax.ShapeDtypeStruct((n_rows,), jnp.int32),
        name="invert",
    )(dest_flat)


def _moe(x1, x1t, info, info_t, prm, l):
    T = x1.shape[0]
    tm = MOE_TM
    n_blk = (2 * T) // tm + N_EXPERTS
    ranks, cnt = _rank(info_t, prm["tri_rank"])
    counts = cnt[:, 0].astype(jnp.int32)
    padded = ((counts + tm - 1) // tm) * tm
    pad_end = jnp.cumsum(padded)
    pad_off = pad_end - padded
    eid = info_t[0:2].astype(jnp.int32)
    e_iota = jnp.arange(N_EXPERTS, dtype=jnp.int32)
    off = jnp.sum(jnp.where(eid[:, None, :] == e_iota[None, :, None], pad_off[None, :, None], 0), axis=1)
    dest = off + ranks[0:2].astype(jnp.int32)
    src_tok = _invert(dest.reshape(-1), n_blk * tm)
    blk_start = jnp.arange(n_blk, dtype=jnp.int32) * tm
    block_expert = jnp.minimum(jnp.sum((pad_end[None, :] <= blk_start[:, None]).astype(jnp.int32), axis=1),
                               N_EXPERTS - 1)
    n_used = (pad_end[-1] // tm).astype(jnp.int32).reshape(1)
    block_expert = jnp.where(jnp.arange(n_blk) < n_used[0], block_expert,
                             block_expert[jnp.maximum(n_used[0] - 1, 0)])
    flat = lambda w: w.reshape((-1,) + w.shape[2:])
    y_pad = _experts(block_expert + l * N_EXPERTS, n_used, src_tok, x1t,
                     flat(prm["moe_w1"]), flat(prm["moe_w3"]), flat(prm["moe_w2"]))
    ct = COMB_TM
    dest_blk = dest.reshape(2, T // ct, ct).transpose(1, 0, 2).reshape(T // ct, 1, 2 * ct)
    return _combine(dest_blk, y_pad, info, x1, prm["ln2_g"][l], prm["ln2_b"][l])


def _prep_params(p):
    L = p["w_in"].shape[0]
    w = p["w_in"]
    fg = jnp.pad(w[:, :, 1536:1544], ((0, 0), (0, 0), (0, LANES - FOX_HEADS)))
    w_in = jnp.concatenate([w[:, :, 0:1536], fg, w[:, :, 1544:2568], w[:, :, 2568:3592]], axis=-1)
    out = {
        "w_in": w_in.astype(BF16),
        "fbias": jnp.pad(p["fox_fgate_bias"], ((0, 0), (0, LANES - FOX_HEADS)))[:, None, :],
        "tri_proj": jnp.tril(jnp.ones((PROJ_TM, PROJ_TM), BF16)),
    }
    vec = jnp.stack([p["rwkv_w0"], p["rwkv_a0"], p["rwkv_k_k"], p["rwkv_k_a"],
                     p["rwkv_r_k"].reshape(L, RWKV_W), p["rwkv_ln_g"], p["rwkv_ln_b"],
                     jnp.zeros((L, RWKV_W), F32)], axis=1)
    ti = np.arange(RWKV_TS)
    lt = ((ti[:, None] // RWKV_C == ti[None, :] // RWKV_C) & (ti[None, :] <= ti[:, None])).astype(np.float32)
    hi = np.arange(RWKV_W) // HEAD_DIM
    out.update({
        "rwkv_mu": jnp.concatenate([p["rwkv_mu_rkv"].reshape(L, 3 * RWKV_W), p["rwkv_mu_lora"]], axis=-1)[:, None, :],
        "rwkv_vec": vec,
        "rwkv_w2": p["rwkv_w2"], "rwkv_a2": p["rwkv_a2"], "rwkv_g2": p["rwkv_g2"],
        "rwkv_lt": jnp.asarray(lt, dtype=BF16),
        "block_ones": jnp.asarray((hi[:, None] == hi[None, :]).astype(np.float32), dtype=BF16),
    })
    out.update(_ret_tables(p["x"].shape[1]))
    out["ret_gn"] = p["ret_gn_g"][:, None, :]
    pad_r = LANES - N_GROUPS - N_EXPERTS
    ri = np.arange(RANK_TM)
    out.update({
        "w_out": p["w_out"].astype(BF16),
        "ln1_g": p["ln1_g"][:, None, :], "ln1_b": p["ln1_b"][:, None, :],
        "ln2_g": p["ln2_g"][:, None, :], "ln2_b": p["ln2_b"][:, None, :],
        "w_router": jnp.pad(jnp.concatenate([p["moe_w_group"], p["moe_w_expert"]], axis=-1),
                            ((0, 0), (0, 0), (0, pad_r))),
        "b_router": jnp.pad(jnp.concatenate([p["moe_b_group"], p["moe_b_expert"]], axis=-1),
                            ((0, 0), (0, pad_r)))[:, None, :],
        "tri_rank": jnp.asarray((ri[:, None] < ri[None, :]).astype(np.float32), dtype=BF16),
        "moe_w1": p["moe_w1"], "moe_w3": p["moe_w3"], "moe_w2": p["moe_w2"],
    })
    return out


def kernel(x, w_in, fox_fgate_bias, rwkv_mu_rkv, rwkv_mu_lora, rwkv_w0, rwkv_w2, rwkv_a0, rwkv_a2, rwkv_g2,
           rwkv_k_k, rwkv_k_a, rwkv_r_k, rwkv_ln_g, rwkv_ln_b, ret_gn_g, w_out, ln1_g, ln1_b, ln2_g, ln2_b,
           moe_w_group, moe_b_group, moe_w_expert, moe_b_expert, moe_w1, moe_w3, moe_w2):
    prm = _prep_params(dict(
        x=x, w_in=w_in, fox_fgate_bias=fox_fgate_bias, rwkv_mu_rkv=rwkv_mu_rkv, rwkv_mu_lora=rwkv_mu_lora,
        rwkv_w0=rwkv_w0, rwkv_w2=rwkv_w2, rwkv_a0=rwkv_a0, rwkv_a2=rwkv_a2, rwkv_g2=rwkv_g2,
        rwkv_k_k=rwkv_k_k, rwkv_k_a=rwkv_k_a, rwkv_r_k=rwkv_r_k, rwkv_ln_g=rwkv_ln_g, rwkv_ln_b=rwkv_ln_b,
        ret_gn_g=ret_gn_g, w_out=w_out, ln1_g=ln1_g, ln1_b=ln1_b, ln2_g=ln2_g, ln2_b=ln2_b,
        moe_w_group=moe_w_group, moe_b_group=moe_b_group, moe_w_expert=moe_w_expert,
        moe_b_expert=moe_b_expert, moe_w1=moe_w1, moe_w3=moe_w3, moe_w2=moe_w2))
    B, S, D = x.shape
    T = B * S
    for l in range(w_in.shape[0]):
        qa, ka, va, rw, rt, stats = _project(x, prm["w_in"][l], prm["fbias"][l], prm["tri_proj"])
        y_fox = _fox_attention(qa, ka, va, stats)
        y_rwkv = _rwkv(rw, prm, l)
        y_ret = _retention(rt, prm, l)
        x1, info, info_t = _out_proj(y_fox.reshape(T, FOX_W), y_rwkv.reshape(T, RWKV_W),
                                     y_ret.reshape(T, RET_W), x.reshape(T, D), prm, l)
        x = _moe(x1, x1.reshape(T, ROW_TILES, LANES), info, info_t, prm, l).reshape(B, S, D)
    return x
```
